```python
import math
import jax, jax.numpy as jnp
from jax import lax
import numpy as np

D_MODEL = 1024
BATCH = 4
SEQ = 8192
DEPTH = 4

CHUNK = 64
N_META = 16
Q_BLOCK = 128
ROPE_THETA = 500000.0
NORM_EPS = 1e-6

POOL_WINDOWS = (2, 4, 8, 16)
POOL_GROUP = D_MODEL // 8
POOL_WIDTH = POOL_GROUP * len(POOL_WINDOWS)

DA_HEADS = 4
DA_DIM = 64
DA_VDIM = 2 * DA_DIM
DA_QK = DA_HEADS * 2 * DA_DIM
DA_WIDTH = DA_HEADS * DA_VDIM
DA_ROPE = DA_DIM // 4
DA_SUBLN_EPS = 1e-5

RW_HEAD = 64
RW_WIDTH = D_MODEL // 2
RW_HEADS = RW_WIDTH // RW_HEAD
RW_DECAY_LORA = 64
RW_A_LORA = 64
RW_GATE_LORA = 128
RW_SHIFT_WIDTH = 3 * RW_WIDTH + RW_DECAY_LORA + RW_A_LORA + RW_GATE_LORA
RW_SPLITS = (RW_WIDTH, 2 * RW_WIDTH, 3 * RW_WIDTH, 3 * RW_WIDTH + RW_DECAY_LORA,
             3 * RW_WIDTH + RW_DECAY_LORA + RW_A_LORA)
RW_GN_EPS = 64e-5

N_BRANCH = 3
BRANCH_WIDTH = 512
IN_PARTS = (POOL_WIDTH, DA_QK, DA_QK, DA_WIDTH, RW_SHIFT_WIDTH, N_BRANCH * D_MODEL)
IN_WIDTH = sum(IN_PARTS)
IN_SPLITS = tuple(int(s) for s in np.cumsum(IN_PARTS)[:-1])

D_FF = 2816
CONV_WIDTH = 3

kernel_name = "hybrid_pool_diffattn_rwkv7_stream_block"


def rms_norm(x, g, eps=NORM_EPS):
    xf = x.astype(jnp.float32)
    y = xf * lax.rsqrt(jnp.mean(xf * xf, axis=-1, keepdims=True) + eps)
    return (y * g.astype(jnp.float32)).astype(x.dtype)


def rope_tables(n):
    pos = jnp.arange(n, dtype=jnp.float32)
    inv = ROPE_THETA ** (-jnp.arange(0, DA_ROPE, 2, dtype=jnp.float32) / DA_ROPE)
    ang = pos[:, None] * inv[None, :]
    return jnp.cos(ang), jnp.sin(ang)


def partial_rope(t, cos, sin):
    half = DA_ROPE // 2
    c = cos[None, :, None, None, :].astype(t.dtype)
    s = sin[None, :, None, None, :].astype(t.dtype)
    t1 = t[..., :half]
    t2 = t[..., half:DA_ROPE]
    return jnp.concatenate([t1 * c - t2 * s, t2 * c + t1 * s, t[..., DA_ROPE:]], axis=-1)


def pool_mixer(u, w_group, scale):
    B, L, _ = u.shape
    uf = u.astype(jnp.float32).reshape(B, L, len(POOL_WINDOWS), POOL_GROUP)
    cs = jnp.cumsum(uf, axis=1)
    t1 = jnp.arange(1, L + 1, dtype=jnp.float32)
    outs = []
    for g, w in enumerate(POOL_WINDOWS):
        c = cs[:, :, g]
        lag = jnp.pad(c, ((0, 0), (w, 0), (0, 0)))[:, :L]
        cnt = jnp.minimum(t1, float(w))[None, :, None]
        outs.append((c - lag) / cnt - uf[:, :, g])
    pooled = jnp.stack(outs, axis=2).astype(u.dtype)
    mixed = jnp.einsum('blgc,gcd->blgd', pooled, w_group)
    return mixed.reshape(B, L, POOL_WIDTH) * scale


def diff_attention(q, k, v, lam, lambda_init, subln_w):
    B, L = q.shape[:2]
    n_blk = -(-L // Q_BLOCK)
    Lp = n_blk * Q_BLOCK
    pad5 = ((0, 0), (0, Lp - L), (0, 0), (0, 0), (0, 0))
    q = jnp.pad(q, pad5)
    k = jnp.pad(k, pad5)
    v = jnp.pad(v, pad5[:4])
    chunk = (jnp.arange(Lp) - N_META) // CHUNK
    qb = q.reshape(B, n_blk, Q_BLOCK, DA_HEADS, 2, DA_DIM).transpose(1, 0, 2, 3, 4, 5)
    cb = chunk.reshape(n_blk, Q_BLOCK)
    scale = DA_DIM ** -0.5

    def block(args):
        qi, ci = args
        s = jnp.einsum('bqhcd,bkhcd->bhcqk', qi, k).astype(jnp.float32) * scale
        mask = chunk[None, :] <= ci[:, None]
        s = jnp.where(mask, s, -jnp.inf)
        p = jax.nn.softmax(s, axis=-1)
        a = p[:, :, 0] - lam * p[:, :, 1]
        return jnp.einsum('bhqk,bkhe->bqhe', a.astype(v.dtype), v)

    o = lax.map(block, (qb, cb))
    o = o.transpose(1, 0, 2, 3, 4).reshape(B, Lp, DA_HEADS, DA_VDIM)[:, :L]
    o = rms_norm(o, subln_w, DA_SUBLN_EPS) * (1.0 - lambda_init)
    return o.reshape(B, L, DA_WIDTH)


def token_shift(t):
    return jnp.pad(t, ((0, 0), (1, 0), (0, 0)))[:, :-1]


def rwkv7_mixer(p, mu, w0, w2, a0, a2, g2, k_k, k_a, r_k, lnx_w, lnx_b):
    B, L, _ = p.shape
    f32 = jnp.float32
    p = p + mu * (token_shift(p) - p)
    r, k, v, wl, al, gl = jnp.split(p, RW_SPLITS, axis=-1)
    w_log = -jax.nn.softplus(-(w0 + jnp.tanh(wl) @ w2)) - 0.5
    decay = jnp.exp(-jnp.exp(w_log.astype(f32)))
    a = jax.nn.sigmoid(a0 + al @ a2)
    g = jax.nn.sigmoid(gl) @ g2
    heads = lambda t: t.reshape(B, L, RW_HEADS, RW_HEAD).astype(f32)
    kk = heads(k * k_k)
    kk = kk / jnp.maximum(jnp.sqrt(jnp.sum(kk * kk, axis=-1, keepdims=True)), 1e-12)
    k = k * (1.0 + (a - 1.0) * k_a)
    rh, kh, vh, ah, dh = heads(r), heads(k), heads(v), heads(a), heads(decay)
    rem = -kk
    rep = kk * ah

    def step(S, inp):
        r_t, w_t, k_t, v_t, a_t, b_t = inp
        S = (S * w_t[:, :, None, :]
             + jnp.einsum('bhvk,bhk->bhv', S, a_t)[..., None] * b_t[:, :, None, :]
             + v_t[..., None] * k_t[:, :, None, :])
        return S, jnp.einsum('bhvk,bhk->bhv', S, r_t)

    xs = tuple(t.transpose(1, 0, 2, 3) for t in (rh, dh, kh, vh, rem, rep))
    S0 = jnp.zeros((B, RW_HEADS, RW_HEAD, RW_HEAD), f32)
    _, y = lax.scan(step, S0, xs)
    y = y.transpose(1, 0, 2, 3)
    mean = jnp.mean(y, axis=-1, keepdims=True)
    var = jnp.mean(jnp.square(y - mean), axis=-1, keepdims=True)
    y = ((y - mean) * lax.rsqrt(var + RW_GN_EPS)).reshape(B, L, RW_WIDTH)
    y = y * lnx_w.astype(f32) + lnx_b.astype(f32)
    bonus = jnp.sum(rh * kh * r_k.astype(f32), axis=-1, keepdims=True) * vh
    y = (y + bonus.reshape(B, L, RW_WIDTH)) * g.astype(f32)
    return y.astype(p.dtype)


def conv_glu_ffn(h, w_up, conv_w, w_down):
    u = h @ w_up
    C = u.shape[-1]
    u = lax.conv_general_dilated(u, conv_w[:, None, :].astype(u.dtype), window_strides=(1,),
                                 padding=[(CONV_WIDTH - 1, 0)],
                                 dimension_numbers=('NWC', 'WIO', 'NWC'),
                                 feature_group_count=C)
    gate, up = jnp.split(u, 2, axis=-1)
    return (jax.nn.silu(gate) * up) @ w_down


def setup_inputs(seed: int = 0) -> dict:
    key = jax.random.key(seed)
    ks = iter(jax.random.split(key, 32))
    f32 = jnp.float32
    nrm = lambda shape, s: jax.random.normal(next(ks), shape, f32) * s
    uni = lambda shape, lo, hi: jax.random.uniform(next(ks), shape, f32, lo, hi)
    Ld = DEPTH
    return {
        "x": nrm((BATCH, SEQ, D_MODEL), 1.0),
        "meta_tokens": nrm((N_META, D_MODEL), 1.0),
        "norm_mix": 1.0 + nrm((Ld, D_MODEL), 0.02),
        "norm_ffn": 1.0 + nrm((Ld, D_MODEL), 0.02),
        "norm_final": 1.0 + nrm((D_MODEL,), 0.02),
        "w_in": nrm((Ld, D_MODEL, IN_WIDTH), D_MODEL ** -0.5),
        "pool_w": nrm((Ld, len(POOL_WINDOWS), POOL_GROUP, POOL_GROUP), POOL_GROUP ** -0.5),
        "pool_scale": 1.0 + nrm((Ld, POOL_WIDTH), 0.02),
        "da_lambda": nrm((Ld, 4, DA_DIM), 0.1),
        "da_subln": 1.0 + nrm((Ld, DA_VDIM), 0.02),
        "rw_mu": uni((Ld, RW_SHIFT_WIDTH), 0.0, 1.0),
        "rw_w0": uni((Ld, RW_WIDTH), -5.0, 1.0),
        "rw_w2": nrm((Ld, RW_DECAY_LORA, RW_WIDTH), 0.05),
        "rw_a0": nrm((Ld, RW_WIDTH), 0.5),
        "rw_a2": nrm((Ld, RW_A_LORA, RW_WIDTH), 0.1),
        "rw_g2": nrm((Ld, RW_GATE_LORA, RW_WIDTH), RW_GATE_LORA ** -0.5),
        "rw_k_k": 0.85 + nrm((Ld, RW_WIDTH), 0.05),
        "rw_k_a": 1.0 + nrm((Ld, RW_WIDTH), 0.05),
        "rw_r_k": nrm((Ld, RW_HEADS, RW_HEAD), 0.1),
        "rw_lnx_w": 1.0 + nrm((Ld, RW_WIDTH), 0.02),
        "rw_lnx_b": nrm((Ld, RW_WIDTH), 0.02),
        "w_branch": nrm((Ld, N_BRANCH, BRANCH_WIDTH, D_MODEL), BRANCH_WIDTH ** -0.5),
        "w_out": nrm((Ld, D_MODEL, D_MODEL), D_MODEL ** -0.5),
        "ffn_up": nrm((Ld, D_MODEL, 2 * D_FF), D_MODEL ** -0.5),
        "ffn_conv": nrm((Ld, CONV_WIDTH, 2 * D_FF), 0.5),
        "ffn_down": nrm((Ld, D_FF, D_MODEL), D_FF ** -0.5),
    }


def reference(x, meta_tokens, norm_mix, norm_ffn, norm_final, w_in, pool_w, pool_scale,
              da_lambda, da_subln, rw_mu, rw_w0, rw_w2, rw_a0, rw_a2, rw_g2, rw_k_k, rw_k_a,
              rw_r_k, rw_lnx_w, rw_lnx_b, w_branch, w_out, ffn_up, ffn_conv, ffn_down):
    B = x.shape[0]
    meta = jnp.broadcast_to(meta_tokens[None].astype(x.dtype), (B, N_META, D_MODEL))
    x = jnp.concatenate([meta, x], axis=1)
    L = x.shape[1]
    cos, sin = rope_tables(L)
    for l in range(DEPTH):
        lambda_init = 0.8 - 0.6 * math.exp(-0.3 * l)
        h = rms_norm(x, norm_mix[l])
        proj = h @ w_in[l]
        u_pool, q, k, v, p_rw, gate_logits = jnp.split(proj, IN_SPLITS, axis=-1)
        b_pool = pool_mixer(u_pool, pool_w[l], pool_scale[l])
        q = partial_rope(q.reshape(B, L, DA_HEADS, 2, DA_DIM), cos, sin)
        k = partial_rope(k.reshape(B, L, DA_HEADS, 2, DA_DIM), cos, sin)
        v = v.reshape(B, L, DA_HEADS, DA_VDIM)
        lq1, lk1, lq2, lk2 = [da_lambda[l, i].astype(jnp.float32) for i in range(4)]
        lam = jnp.exp(jnp.sum(lq1 * lk1)) - jnp.exp(jnp.sum(lq2 * lk2)) + lambda_init
        b_da = diff_attention(q, k, v, lam, lambda_init, da_subln[l])
        b_rw = rwkv7_mixer(p_rw, rw_mu[l], rw_w0[l], rw_w2[l], rw_a0[l], rw_a2[l], rw_g2[l],
                           rw_k_k[l], rw_k_a[l], rw_r_k[l], rw_lnx_w[l], rw_lnx_b[l])
        branches = jnp.stack([b_pool, b_da, b_rw], axis=2)
        up = jnp.einsum('blnc,ncd->blnd', branches, w_branch[l])
        gates = jax.nn.sigmoid(gate_logits.reshape(B, L, N_BRANCH, D_MODEL).astype(jnp.float32))
        merged = jnp.sum(gates.astype(up.dtype) * up, axis=2)
        x = x + merged @ w_out[l]
        x = x + conv_glu_ffn(rms_norm(x, norm_ffn[l]), ffn_up[l], ffn_conv[l], ffn_down[l])
    return rms_norm(x, norm_final)[:, N_META:]
```

```python
import functools
import math

import jax
import jax.numpy as jnp
from jax import lax
from jax.experimental import pallas as pl
from jax.experimental.pallas import tpu as pltpu

F32 = jnp.float32
BF16 = jnp.bfloat16

D_MODEL = 1024
N_META = 16
CHUNK = 64
ROPE_THETA = 500000.0
NORM_EPS = 1e-6
POOL_WINDOWS = (2, 4, 8, 16)
POOL_GROUP = 128
POOL_WIDTH = 512
DA_HEADS = 4
DA_DIM = 64
DA_VDIM = 128
DA_ROPE = 16
DA_SUBLN_EPS = 1e-5
RW_HEAD = 64
RW_WIDTH = 512
RW_HEADS = 8
RW_LORA_SLAB = 128
RW_GATE_LORA = 128
RW_SHIFT_WIDTH = 1792
RW_GN_EPS = 64e-5
N_BRANCH = 3
D_FF = 2816
FF_CHUNK = 1408

LANES = 128
SUBLANES = 8
VMEM_LIMIT_BYTES = 56 * 1024 * 1024

ATTN_TILE = 128
PAD_FRONT = ATTN_TILE - N_META
MASK_VALUE = -1e30
POOL_HALO = 16
ROW_HALO = SUBLANES

NN = (((1,), (0,)), ((), ()))
NT = (((1,), (1,)), ((), ()))


def _dot(a, b, dn=NN):
    return lax.dot_general(a, b, dn, preferred_element_type=F32)


def _split(x):
    hi = x.astype(BF16)
    lo = (x - hi.astype(F32)).astype(BF16)
    return hi, lo


def _mm3(a, b, dn=NN):
    ah, al = _split(a)
    bh, bl = _split(b)
    return _dot(ah, bh, dn) + _dot(ah, bl, dn) + _dot(al, bh, dn)


def _mm2_exact_rhs(a, b_bf16):
    ah, al = _split(a)
    return _dot(ah, b_bf16) + _dot(al, b_bf16)


def _mm2_exact_lhs(a_bf16, b):
    bh, bl = _split(b)
    return _dot(a_bf16, bh) + _dot(a_bf16, bl)


def _rms(x, g, eps):
    ms = jnp.mean(x * x, axis=-1, keepdims=True)
    return x * lax.rsqrt(ms + eps) * g


def _row_in_batch(tm, tiles_per_batch, shape):
    t = pl.program_id(0) % tiles_per_batch
    return t * tm + lax.broadcasted_iota(jnp.int32, shape, 0)


def _params(n_axes=1, parallel=True):
    sem = ("parallel" if parallel else "arbitrary",) * n_axes
    return pltpu.CompilerParams(dimension_semantics=sem, vmem_limit_bytes=VMEM_LIMIT_BYTES)


def _const_spec(shape):
    nd = len(shape)
    return pl.BlockSpec(shape, lambda *_: (0,) * nd)


def _halo_spec(tm, halo, width):
    per = tm // halo
    return pl.BlockSpec((halo, width), lambda i: (jnp.maximum(i * per - 1, 0), 0))


def _pick_tile(lp, candidates):
    for c in candidates:
        if lp % c == 0:
            return c
    raise ValueError(f"no row tile for padded length {lp}")


def _proj_attn_kernel(x_ref, g_ref, w_ref, rc_ref, rs1_ref, rs2_ref, u_ref, q_ref, k_ref, v_ref):
    h = _rms(x_ref[...], g_ref[...], NORM_EPS).astype(BF16)
    p = _dot(h, w_ref[...])
    rep = POOL_WIDTH // LANES
    rc = jnp.concatenate([rc_ref[...]] * rep, axis=1)
    rs1 = jnp.concatenate([rs1_ref[...]] * rep, axis=1)
    rs2 = jnp.concatenate([rs2_ref[...]] * rep, axis=1)

    def rope(t):
        up = pltpu.roll(t, POOL_WIDTH - DA_ROPE // 2, 1)
        dn = pltpu.roll(t, DA_ROPE // 2, 1)
        return t * rc + up * rs1 + dn * rs2

    u_ref[...] = p[:, 0:512]
    q_ref[...] = (rope(p[:, 512:1024]) * (DA_DIM ** -0.5)).astype(BF16)
    k_ref[...] = rope(p[:, 1024:1536]).astype(BF16)
    v_ref[...] = p[:, 1536:2048].astype(BF16)


def _proj_attn(x, g, w, rc, rs1, rs2, lp):
    n = x.shape[0]
    tm = _pick_tile(lp, (640, 320, 128))
    tpb = lp // tm
    row = lambda width: pl.BlockSpec((tm, width), lambda i: (i, 0))
    tab = pl.BlockSpec((tm, LANES), lambda i: (i % tpb, 0))
    return pl.pallas_call(
        _proj_attn_kernel,
        grid=(n // tm,),
        in_specs=[row(D_MODEL), _const_spec((1, D_MODEL)), _const_spec(w.shape), tab, tab, tab],
        out_specs=[row(512)] * 4,
        out_shape=[jax.ShapeDtypeStruct((n, 512), F32)] + [jax.ShapeDtypeStruct((n, 512), BF16)] * 3,
        compiler_params=_params(),
        name="proj_attn",
    )(x, g, w, rc, rs1, rs2)


def _proj_gate_kernel(x_ref, g_ref, w_ref, o_ref):
    h = _rms(x_ref[...], g_ref[...], NORM_EPS).astype(BF16)
    o_ref[...] = _dot(h, w_ref[...])


def _proj_gate(x, g, w, lp):
    n = x.shape[0]
    tm = _pick_tile(lp, (640, 320, 128))
    width = w.shape[1]
    return pl.pallas_call(
        _proj_gate_kernel,
        grid=(n // tm,),
        in_specs=[pl.BlockSpec((tm, D_MODEL), lambda i: (i, 0)), _const_spec((1, D_MODEL)),
                  _const_spec(w.shape)],
        out_specs=pl.BlockSpec((tm, width), lambda i: (i, 0)),
        out_shape=jax.ShapeDtypeStruct((n, width), F32),
        compiler_params=_params(),
        name="proj_gate",
    )(x, g, w)


def _head_sum(x, bd):
    parts = [_mm2_exact_rhs(x[:, s:s + LANES], bd) for s in range(0, x.shape[1], LANES)]
    return jnp.concatenate(parts, axis=1)


def _rw_prep_kernel(tpb, x_ref, xh_ref, g_ref, w_ref, mu_ref, w0_ref, w2_ref, a0_ref, a2_ref, g2_ref,
                    kk_ref, ka_ref, bd_ref, r_o, lw_o, k_o, v_o, kk_o, a_o, g_o):
    tm = x_ref.shape[0]
    x_ext = jnp.concatenate([xh_ref[...], x_ref[...]], axis=0)
    h = _rms(x_ext, g_ref[...], NORM_EPS).astype(BF16)
    p_ext = _dot(h, w_ref[...])
    p = p_ext[ROW_HALO:]
    p_prev = pltpu.roll(p_ext, 1, 0)[ROW_HALO:]
    valid = _row_in_batch(tm, tpb, (tm, 1)) >= PAD_FRONT
    pm = jnp.where(valid, p + mu_ref[...] * (p_prev - p), 0.0)

    r = pm[:, 0:512]
    k = pm[:, 512:1024]
    v = pm[:, 1024:1536]
    lora = pm[:, 1536:1536 + RW_LORA_SLAB]
    gl = pm[:, 1536 + RW_LORA_SLAB:RW_SHIFT_WIDTH]

    z = w0_ref[...] + _dot(jnp.tanh(lora).astype(BF16), w2_ref[...])
    softplus_neg = jnp.maximum(-z, 0.0) + jnp.log1p(jnp.exp(-jnp.abs(z)))
    w_log = -softplus_neg - 0.5
    lw = jnp.where(valid, -jnp.exp(w_log), 0.0)
    a = jax.nn.sigmoid(a0_ref[...] + _dot(lora.astype(BF16), a2_ref[...]))
    g = _dot(jax.nn.sigmoid(gl).astype(BF16), g2_ref[...])
    kk = k * kk_ref[...]
    nrm = jnp.sqrt(_head_sum(kk * kk, bd_ref[...]))
    kk = kk / jnp.maximum(nrm, 1e-12)
    k2 = k * (1.0 + (a - 1.0) * ka_ref[...])

    r_o[...] = r
    lw_o[...] = lw
    k_o[...] = k2
    v_o[...] = v
    kk_o[...] = kk
    a_o[...] = a
    g_o[...] = g


def _rw_prep(x, g, w, mu, w0, w2p, a0, a2p, g2, k_k, k_a, bd, lp):
    n = x.shape[0]
    tm = _pick_tile(lp, (320, 128))
    tpb = lp // tm
    vec = lambda width: _const_spec((1, width))
    row512 = pl.BlockSpec((tm, 512), lambda i: (i, 0))
    return pl.pallas_call(
        functools.partial(_rw_prep_kernel, tpb),
        grid=(n // tm,),
        in_specs=[pl.BlockSpec((tm, D_MODEL), lambda i: (i, 0)), _halo_spec(tm, ROW_HALO, D_MODEL),
                  vec(D_MODEL), _const_spec(w.shape), vec(RW_SHIFT_WIDTH), vec(512),
                  _const_spec(w2p.shape), vec(512), _const_spec(a2p.shape), _const_spec(g2.shape),
                  vec(512), vec(512), _const_spec(bd.shape)],
        out_specs=[row512] * 7,
        out_shape=[jax.ShapeDtypeStruct((n, 512), F32)] * 7,
        compiler_params=_params(),
        name="rw_prep",
    )(x, x, g, w, mu, w0, w2p, a0, a2p, g2, k_k, k_a, bd)


def _unit_lower_inverse(nmat, masks):
    eye, blk16, m32, m64 = masks
    nd = jnp.where(blk16, nmat, 0.0)
    t = eye + nd
    p = nd
    for _ in range(3):
        p = _mm3(p, p)
        t = t + _mm3(t, p)
    for m in (m32, m64):
        t = t + _mm3(_mm3(t, jnp.where(m, nmat, 0.0)), t)
    return t


def _rwkv_kernel(r_ref, lw_ref, k_ref, v_ref, kk_ref, a_ref, g_ref, lnw_ref, lnb_ref, rk_ref, bd_ref,
                 o_ref, s_ref, y_ref):
    tb = r_ref.shape[0]
    c = CHUNK

    @pl.when(pl.program_id(1) == 0)
    def _():
        s_ref[...] = jnp.zeros_like(s_ref)

    ri = lax.broadcasted_iota(jnp.int32, (c, c), 0)
    ci = lax.broadcasted_iota(jnp.int32, (c, c), 1)
    strict = ri > ci
    incl = ri >= ci
    eye = jnp.where(ri == ci, 1.0, 0.0).astype(F32)
    blk16 = (ri // 16) == (ci // 16)
    m32 = ((ri // 32) == (ci // 32)) & ((ri // 16) > (ci // 16))
    m64 = (ri >= 32) & (ci < 32)
    masks = (eye, blk16, m32, m64)
    ltri = jnp.where(incl, 1.0, 0.0).astype(BF16)

    def chunk_body(ic, carry):
        rows = pl.ds(pl.multiple_of(ic * c, c), c)
        r = r_ref[rows, :]
        lw = lw_ref[rows, :]
        k = k_ref[rows, :]
        v = v_ref[rows, :]
        kk = kk_ref[rows, :]
        a = a_ref[rows, :]
        cum = _mm2_exact_lhs(ltri, lw)
        w_t = jnp.exp(cum)
        w_prev = jnp.exp(cum - lw)
        w_inv = jnp.exp(-cum)
        w_end = w_t[c - 1:c, :]
        at = -kk * w_prev
        rt = r * w_t
        bt = kk * a * w_inv
        kt = k * w_inv
        b_end = bt * w_end
        k_end = kt * w_end
        ys = []
        for hd in range(RW_HEADS):
            sl = slice(hd * RW_HEAD, (hd + 1) * RW_HEAD)
            s0 = s_ref[hd]
            lhs = jnp.concatenate([at[:, sl], rt[:, sl]], axis=0)
            gb = _mm3(lhs, bt[:, sl], NT)
            gk = _mm3(lhs, kt[:, sl], NT)
            st = _mm3(lhs, s0, NT)
            a_ab = jnp.where(strict, gb[:c], 0.0)
            a_ak = jnp.where(strict, gk[:c], 0.0)
            a_rb = jnp.where(incl, gb[c:], 0.0)
            a_rk = jnp.where(incl, gk[c:], 0.0)
            vh = v[:, sl]
            tinv = _unit_lower_inverse(a_ab, masks)
            u = _mm3(tinv, st[:c] + _mm3(a_ak, vh))
            ys.append(st[c:] + _mm3(a_rb, u) + _mm3(a_rk, vh))
            uv_t = jnp.transpose(jnp.concatenate([u, vh], axis=0))
            bk = jnp.concatenate([b_end[:, sl], k_end[:, sl]], axis=0)
            s_ref[hd] = s0 * w_end[:, sl] + _mm3(uv_t, bk)
        y_ref[rows, :] = jnp.concatenate(ys, axis=1)
        return carry

    lax.fori_loop(0, tb // c, chunk_body, 0)

    y = y_ref[...]
    bd = bd_ref[...]
    inv_n = 1.0 / RW_HEAD
    mean = _head_sum(y, bd) * inv_n
    d = y - mean
    var = _head_sum(d * d, bd) * inv_n
    yn = d * lax.rsqrt(var + RW_GN_EPS) * lnw_ref[...] + lnb_ref[...]
    r = r_ref[...]
    bonus = _head_sum(r * k_ref[...] * rk_ref[...], bd) * v_ref[...]
    o_ref[...] = ((yn + bonus) * g_ref[...]).astype(o_ref.dtype)


def _rwkv(r, lw, k2, v, kk, a, g, lnw, lnb, rk, bd, batch, lp):
    n = r.shape[0]
    tb = _pick_tile(lp, (640, 320, 128))
    nblk = lp // tb
    row = pl.BlockSpec((tb, 512), lambda b, j: (b * nblk + j, 0))
    vec = _const_spec((1, 512))
    return pl.pallas_call(
        _rwkv_kernel,
        grid=(batch, nblk),
        in_specs=[row] * 7 + [vec, vec, vec, _const_spec(bd.shape)],
        out_specs=row,
        out_shape=jax.ShapeDtypeStruct((n, 512), BF16),
        scratch_shapes=[pltpu.VMEM((RW_HEADS, RW_HEAD, RW_HEAD), F32), pltpu.VMEM((tb, 512), F32)],
        compiler_params=_params(2, parallel=False),
        name="rwkv_chunk",
    )(r, lw, k2, v, kk, a, g, lnw, lnb, rk, bd)


def _attn_kernel(lambda_init, q_ref, k_ref, v_ref, dl_ref, sub_ref, o_ref, qq_ref, m_ref, l_ref, acc_ref):
    t = ATTN_TILE
    qi = pl.program_id(2)
    q = q_ref[...]
    lane = lax.broadcasted_iota(jnp.int32, (t, 2 * DA_DIM), 1)
    zero = jnp.zeros_like(q)
    qq_ref[0:t, :] = jnp.where(lane < DA_DIM, q, zero)
    qq_ref[t:2 * t, :] = jnp.where(lane >= DA_DIM, q, zero)
    m_ref[...] = jnp.full_like(m_ref, MASK_VALUE)
    l_ref[...] = jnp.zeros_like(l_ref)
    acc_ref[...] = jnp.zeros_like(acc_ref)

    def step(j, masked):
        rows = pl.ds(pl.multiple_of(j * t, t), t)
        s = _dot(qq_ref[...], k_ref[rows, :], NT)
        if masked:
            qrow = qi * t + lax.broadcasted_iota(jnp.int32, (2 * t, t), 0) % t
            krow = j * t + lax.broadcasted_iota(jnp.int32, (2 * t, t), 1)
            vis = ((krow // CHUNK) <= (qrow // CHUNK)) & (krow >= PAD_FRONT)
            s = jnp.where(vis, s, MASK_VALUE)
        m_prev = m_ref[...]
        m_new = jnp.maximum(m_prev, jnp.max(s, axis=1, keepdims=True))
        alpha = jnp.exp(m_prev - m_new)
        p = jnp.exp(s - m_new)
        l_ref[...] = alpha * l_ref[...] + jnp.sum(p, axis=1, keepdims=True)
        acc_ref[...] = alpha * acc_ref[...] + _dot(p.astype(BF16), v_ref[rows, :])
        m_ref[...] = m_new

    step(0, True)

    def body(j, carry):
        step(j, False)
        return carry

    lax.fori_loop(1, qi, body, 0)

    @pl.when(qi > 0)
    def _():
        step(qi, True)

    dl = dl_ref[...]
    lam = (jnp.exp(jnp.sum(dl[0:1] * dl[1:2], axis=1, keepdims=True))
           - jnp.exp(jnp.sum(dl[2:3] * dl[3:4], axis=1, keepdims=True)) + lambda_init)
    o = acc_ref[...] / l_ref[...]
    o = o[0:t] - lam * o[t:2 * t]
    o_ref[...] = (_rms(o, sub_ref[...], DA_SUBLN_EPS) * (1.0 - lambda_init)).astype(o_ref.dtype)


def _attention(q, k, v, dl, subln, lambda_init, batch, lp):
    n = q.shape[0]
    t = ATTN_TILE
    nq = lp // t
    qspec = pl.BlockSpec((t, 2 * DA_DIM), lambda b, h, i: (b * nq + i, h))
    kvspec = pl.BlockSpec((lp, 2 * DA_DIM), lambda b, h, i: (b, h))
    return pl.pallas_call(
        functools.partial(_attn_kernel, lambda_init),
        grid=(batch, DA_HEADS, nq),
        in_specs=[qspec, kvspec, kvspec, _const_spec(dl.shape), _const_spec((1, DA_VDIM))],
        out_specs=qspec,
        out_shape=jax.ShapeDtypeStruct((n, 512), BF16),
        scratch_shapes=[pltpu.VMEM((2 * t, 2 * DA_DIM), BF16), pltpu.VMEM((2 * t, 1), F32),
                        pltpu.VMEM((2 * t, 1), F32), pltpu.VMEM((2 * t, DA_VDIM), F32)],
        compiler_params=_params(3),
        name="diff_attn",
    )(q, k, v, dl, subln)


def _merge_kernel(tpb, x_ref, u_ref, uh_ref, da_ref, rw_ref, gate_ref, pw_ref, ps_ref, wb_ref, wo_ref, o_ref):
    tm = x_ref.shape[0]
    row = _row_in_batch(tm, tpb, (tm, 1))
    valid = row >= PAD_FRONT
    pos1 = (row - PAD_FRONT + 1).astype(F32)
    u = u_ref[...]
    ext = jnp.concatenate([uh_ref[...], u], axis=0)
    mixed = []
    for gi, w in enumerate(POOL_WINDOWS):
        sl = slice(gi * POOL_GROUP, (gi + 1) * POOL_GROUP)
        s = ext[:, sl]
        shift = 1
        while shift < w:
            s = s + pltpu.roll(s, shift, 0)
            shift *= 2
        cnt = jnp.clip(pos1, 1.0, float(w))
        pooled = s[POOL_HALO:] / cnt - u[:, sl]
        mixed.append(_dot(pooled.astype(BF16), pw_ref[gi]))
    b_pool = jnp.concatenate(mixed, axis=1) * ps_ref[...]
    gl = gate_ref[...]
    d = D_MODEL
    merged = (jax.nn.sigmoid(gl[:, 0:d]) * _dot(b_pool.astype(BF16), wb_ref[0])
              + jax.nn.sigmoid(gl[:, d:2 * d]) * _dot(da_ref[...], wb_ref[1])
              + jax.nn.sigmoid(gl[:, 2 * d:3 * d]) * _dot(rw_ref[...], wb_ref[2]))
    xn = x_ref[...] + _dot(merged.astype(BF16), wo_ref[...])
    o_ref[...] = jnp.where(valid, xn, 0.0)


def _merge(x, u, da, rw, gates, pw, ps, wb, wo, lp):
    n = x.shape[0]
    tm = _pick_tile(lp, (320, 128))
    tpb = lp // tm
    row = lambda width: pl.BlockSpec((tm, width), lambda i: (i, 0))
    return pl.pallas_call(
        functools.partial(_merge_kernel, tpb),
        grid=(n // tm,),
        in_specs=[row(D_MODEL), row(512), _halo_spec(tm, POOL_HALO, 512), row(512), row(512),
                  row(N_BRANCH * D_MODEL), _const_spec(pw.shape), _const_spec((1, 512)),
                  _const_spec(wb.shape), _const_spec(wo.shape)],
        out_specs=row(D_MODEL),
        out_shape=jax.ShapeDtypeStruct((n, D_MODEL), F32),
        compiler_params=_params(),
        name="merge",
    )(x, u, u, da, rw, gates, pw, ps, wb, wo)


def _ffn_kernel(tpb, x_ref, xh_ref, g_ref, wu_ref, cw_ref, wd_ref, o_ref):
    tm = x_ref.shape[0]
    x = x_ref[...]
    x_ext = jnp.concatenate([xh_ref[...], x], axis=0)
    h = _rms(x_ext, g_ref[...], NORM_EPS).astype(BF16)

    def conv(col):
        u = _dot(h, wu_ref[:, col:col + FF_CHUNK])
        cw = cw_ref[:, col:col + FF_CHUNK]
        out = (u * cw[2:3] + pltpu.roll(u, 1, 0) * cw[1:2] + pltpu.roll(u, 2, 0) * cw[0:1])
        return out[ROW_HALO:]

    acc = x
    for c0 in range(0, D_FF, FF_CHUNK):
        gate = conv(c0)
        up = conv(D_FF + c0)
        act = (gate * jax.nn.sigmoid(gate) * up).astype(BF16)
        acc = acc + _dot(act, wd_ref[c0:c0 + FF_CHUNK, :])
    valid = _row_in_batch(tm, tpb, (tm, 1)) >= PAD_FRONT
    o_ref[...] = jnp.where(valid, acc, 0.0)


def _ffn(x, g, wu, cw, wd, lp):
    n = x.shape[0]
    tm = _pick_tile(lp, (320, 128))
    tpb = lp // tm
    row = pl.BlockSpec((tm, D_MODEL), lambda i: (i, 0))
    return pl.pallas_call(
        functools.partial(_ffn_kernel, tpb),
        grid=(n // tm,),
        in_specs=[row, _halo_spec(tm, ROW_HALO, D_MODEL), _const_spec((1, D_MODEL)),
                  _const_spec(wu.shape), _const_spec(cw.shape), _const_spec(wd.shape)],
        out_specs=row,
        out_shape=jax.ShapeDtypeStruct((n, D_MODEL), F32),
        compiler_params=_params(),
        name="conv_glu_ffn",
    )(x, x, g, wu, cw, wd)


def _final_kernel(x_ref, g_ref, o_ref):
    o_ref[...] = _rms(x_ref[...], g_ref[...], NORM_EPS)


def _final_norm(x, g, batch, lp, seq):
    t = ATTN_TILE
    per_in = lp // t
    per_out = seq // t
    return pl.pallas_call(
        _final_kernel,
        grid=(batch, per_out),
        in_specs=[pl.BlockSpec((t, D_MODEL), lambda b, j: (b * per_in + 1 + j, 0)),
                  _const_spec((1, D_MODEL))],
        out_specs=pl.BlockSpec((t, D_MODEL), lambda b, j: (b * per_out + j, 0)),
        out_shape=jax.ShapeDtypeStruct((batch * seq, D_MODEL), F32),
        compiler_params=_params(2),
        name="final_norm",
    )(x, g)


def _rope_tables(lp):
    half = DA_ROPE // 2
    pos = (jnp.arange(lp, dtype=jnp.int32) - PAD_FRONT).astype(F32)
    inv = ROPE_THETA ** (-jnp.arange(0, DA_ROPE, 2, dtype=F32) / DA_ROPE)
    ang = pos[:, None] * inv[None, :]
    cos, sin = jnp.cos(ang), jnp.sin(ang)
    ones = jnp.ones((lp, DA_DIM - DA_ROPE), F32)
    zeros = jnp.zeros((lp, DA_DIM - DA_ROPE), F32)
    zh = jnp.zeros((lp, half), F32)
    rc = jnp.concatenate([cos, cos, ones], axis=1)
    rs1 = jnp.concatenate([-sin, zh, zeros], axis=1)
    rs2 = jnp.concatenate([zh, sin, zeros], axis=1)
    return tuple(jnp.concatenate([t, t], axis=1) for t in (rc, rs1, rs2))


def kernel(x, meta_tokens, norm_mix, norm_ffn, norm_final, w_in, pool_w, pool_scale, da_lambda, da_subln,
           rw_mu, rw_w0, rw_w2, rw_a0, rw_a2, rw_g2, rw_k_k, rw_k_a, rw_r_k, rw_lnx_w, rw_lnx_b,
           w_branch, w_out, ffn_up, ffn_conv, ffn_down):
    batch, seq, d = x.shape
    depth = w_in.shape[0]
    assert d == D_MODEL and seq % ATTN_TILE == 0 and meta_tokens.shape == (N_META, D_MODEL)
    lp = seq + ATTN_TILE
    n = batch * lp

    meta = jnp.broadcast_to(meta_tokens[None].astype(x.dtype), (batch, N_META, d))
    xs = jnp.concatenate([jnp.zeros((batch, PAD_FRONT, d), x.dtype), meta, x], axis=1).reshape(n, d)

    rc, rs1, rs2 = _rope_tables(lp)
    hi = lax.broadcasted_iota(jnp.int32, (LANES, LANES), 0) // RW_HEAD
    hj = lax.broadcasted_iota(jnp.int32, (LANES, LANES), 1) // RW_HEAD
    bd = (hi == hj).astype(BF16)
    lora_zeros = jnp.zeros((RW_LORA_SLAB // 2, RW_WIDTH), BF16)
    row = lambda v: v.reshape(1, -1).astype(F32)

    for l in range(depth):
        lambda_init = 0.8 - 0.6 * math.exp(-0.3 * l)
        w_l = w_in[l].astype(BF16)
        g_mix = row(norm_mix[l])
        u_pool, q, k, v = _proj_attn(xs, g_mix, w_l[:, 0:2048], rc, rs1, rs2, lp)
        gates = _proj_gate(xs, g_mix, w_l[:, 2048 + RW_SHIFT_WIDTH:], lp)
        w2p = jnp.concatenate([rw_w2[l].astype(BF16), lora_zeros], axis=0)
        a2p = jnp.concatenate([lora_zeros, rw_a2[l].astype(BF16)], axis=0)
        prep = _rw_prep(xs, g_mix, w_l[:, 2048:2048 + RW_SHIFT_WIDTH], row(rw_mu[l]), row(rw_w0[l]), w2p,
                        row(rw_a0[l]), a2p, rw_g2[l].astype(BF16), row(rw_k_k[l]), row(rw_k_a[l]), bd, lp)
        b_rw = _rwkv(*prep, row(rw_lnx_w[l]), row(rw_lnx_b[l]), row(rw_r_k[l]), bd, batch, lp)
        b_da = _attention(q, k, v, da_lambda[l].astype(F32), row(da_subln[l]), lambda_init, batch, lp)
        xs = _merge(xs, u_pool, b_da, b_rw, gates, pool_w[l].astype(BF16), row(pool_scale[l]),
                    w_branch[l].astype(BF16), w_out[l].astype(BF16), lp)
        xs = _ffn(xs, row(norm_ffn[l]), ffn_up[l].astype(BF16), ffn_conv[l].astype(F32),
                  ffn_down[l].astype(BF16), lp)

    out = _final_norm(xs, row(norm_final), batch, lp, seq)
    return out.reshape(batch, seq, d)
```

```python
import functools
import math

import jax
import jax.numpy as jnp
from jax import lax
from jax.experimental import pallas as pl
from jax.experimental.pallas import tpu as pltpu

F32 = jnp.float32
BF16 = jnp.bfloat16

D_MODEL = 1024
N_META = 16
CHUNK = 64
ROPE_THETA = 500000.0
NORM_EPS = 1e-6
POOL_WINDOWS = (2, 4, 8, 16)
POOL_GROUP = 128
POOL_WIDTH = 512
DA_HEADS = 4
DA_DIM = 64
DA_VDIM = 128
DA_ROPE = 16
DA_SUBLN_EPS = 1e-5
RW_HEAD = 64
RW_WIDTH = 512
RW_HEADS = 8
RW_LORA_SLAB = 128
RW_GATE_LORA = 128
RW_SHIFT_WIDTH = 1792
RW_GN_EPS = 64e-5
N_BRANCH = 3
D_FF = 2816
FF_CHUNK = 1408

LANES = 128
SUBLANES = 8
VMEM_LIMIT_BYTES = 56 * 1024 * 1024

ATTN_TILE = 128
ATTN_KV_GROUP = 4
PAD_FRONT = ATTN_TILE - N_META
LOG2_E = math.log2(math.e)
MASK_VALUE = -1e30
POOL_HALO = 16
ROW_HALO = SUBLANES

NN = (((1,), (0,)), ((), ()))
NT = (((1,), (1,)), ((), ()))


def _dot(a, b, dn=NN):
    return lax.dot_general(a, b, dn, preferred_element_type=F32)


def _split(x):
    hi = x.astype(BF16)
    lo = (x - hi.astype(F32)).astype(BF16)
    return hi, lo


def _mm3(a, b, dn=NN):
    ah, al = _split(a)
    bh, bl = _split(b)
    return _dot(ah, bh, dn) + _dot(ah, bl, dn) + _dot(al, bh, dn)


def _mm1(a, b, dn=NN):
    return _dot(a.astype(BF16), b.astype(BF16), dn)


_mm_gram = _mm1
_mm_inv = _mm1
_mm_loc = _mm1
_mm_state = _mm3


def _mm2_exact_rhs(a, b_bf16):
    ah, al = _split(a)
    return _dot(ah, b_bf16) + _dot(al, b_bf16)


def _mm2_exact_lhs(a_bf16, b):
    bh, bl = _split(b)
    return _dot(a_bf16, bh) + _dot(a_bf16, bl)


def _rms(x, g, eps):
    ms = jnp.mean(x * x, axis=-1, keepdims=True)
    return x * lax.rsqrt(ms + eps) * g


def _row_in_batch(tm, tiles_per_batch, shape):
    t = pl.program_id(0) % tiles_per_batch
    return t * tm + lax.broadcasted_iota(jnp.int32, shape, 0)


def _params(n_axes=1, parallel=True):
    sem = ("parallel" if parallel else "arbitrary",) * n_axes
    return pltpu.CompilerParams(dimension_semantics=sem, vmem_limit_bytes=VMEM_LIMIT_BYTES)


def _const_spec(shape):
    nd = len(shape)
    return pl.BlockSpec(shape, lambda *_: (0,) * nd)


def _halo_spec(tm, halo, width):
    per = tm // halo
    return pl.BlockSpec((halo, width), lambda i: (jnp.maximum(i * per - 1, 0), 0))


def _pick_tile(lp, candidates):
    for c in candidates:
        if lp % c == 0:
            return c
    raise ValueError(f"no row tile for padded length {lp}")


def _proj_attn_kernel(x_ref, g_ref, w_ref, wvt_ref, rc_ref, rs1_ref, rs2_ref, u_ref, q_ref, k_ref, vt_ref):
    h = _rms(x_ref[...], g_ref[...], NORM_EPS).astype(BF16)
    p = _dot(h, w_ref[...])
    rep = POOL_WIDTH // LANES
    rc = jnp.concatenate([rc_ref[...]] * rep, axis=1)
    rs1 = jnp.concatenate([rs1_ref[...]] * rep, axis=1)
    rs2 = jnp.concatenate([rs2_ref[...]] * rep, axis=1)

    def rope(t):
        up = pltpu.roll(t, POOL_WIDTH - DA_ROPE // 2, 1)
        dn = pltpu.roll(t, DA_ROPE // 2, 1)
        return t * rc + up * rs1 + dn * rs2

    u_ref[...] = p[:, 0:512]
    q_ref[...] = (rope(p[:, 512:1024]) * (DA_DIM ** -0.5 * LOG2_E)).astype(BF16)
    k_ref[...] = rope(p[:, 1024:1536]).astype(BF16)
    vt_ref[...] = _dot(wvt_ref[...], h, NT).astype(BF16)


def _proj_attn(x, g, w, wvt, rc, rs1, rs2, lp):
    n = x.shape[0]
    tm = _pick_tile(lp, (640, 128))
    tpb = lp // tm
    row = lambda width: pl.BlockSpec((tm, width), lambda i: (i, 0))
    tab = pl.BlockSpec((tm, LANES), lambda i: (i % tpb, 0))
    return pl.pallas_call(
        _proj_attn_kernel,
        grid=(n // tm,),
        in_specs=[row(D_MODEL), _const_spec((1, D_MODEL)), _const_spec(w.shape), _const_spec(wvt.shape),
                  tab, tab, tab],
        out_specs=[row(512)] * 3 + [pl.BlockSpec((512, tm), lambda i: (0, i))],
        out_shape=[jax.ShapeDtypeStruct((n, 512), F32)] + [jax.ShapeDtypeStruct((n, 512), BF16)] * 2
        + [jax.ShapeDtypeStruct((512, n), BF16)],
        compiler_params=_params(),
        name="proj_attn",
    )(x, g, w, wvt, rc, rs1, rs2)


def _proj_gate_kernel(x_ref, g_ref, w_ref, o_ref):
    h = _rms(x_ref[...], g_ref[...], NORM_EPS).astype(BF16)
    o_ref[...] = _dot(h, w_ref[...])


def _proj_gate(x, g, w, lp):
    n = x.shape[0]
    tm = _pick_tile(lp, (640, 320, 128))
    width = w.shape[1]
    return pl.pallas_call(
        _proj_gate_kernel,
        grid=(n // tm,),
        in_specs=[pl.BlockSpec((tm, D_MODEL), lambda i: (i, 0)), _const_spec((1, D_MODEL)),
                  _const_spec(w.shape)],
        out_specs=pl.BlockSpec((tm, width), lambda i: (i, 0)),
        out_shape=jax.ShapeDtypeStruct((n, width), F32),
        compiler_params=_params(),
        name="proj_gate",
    )(x, g, w)


def _head_sum(x, bd):
    parts = [_mm2_exact_rhs(x[:, s:s + LANES], bd) for s in range(0, x.shape[1], LANES)]
    return jnp.concatenate(parts, axis=1)


def _rw_prep_kernel(tpb, x_ref, xh_ref, g_ref, w_ref, mu_ref, w0_ref, w2_ref, a0_ref, a2_ref, g2_ref,
                    kk_ref, ka_ref, bd_ref, r_o, lw_o, k_o, v_o, kk_o, a_o, g_o):
    tm = x_ref.shape[0]
    x_ext = jnp.concatenate([xh_ref[...], x_ref[...]], axis=0)
    h = _rms(x_ext, g_ref[...], NORM_EPS).astype(BF16)
    p_ext = _dot(h, w_ref[...])
    p = p_ext[ROW_HALO:]
    p_prev = pltpu.roll(p_ext, 1, 0)[ROW_HALO:]
    valid = _row_in_batch(tm, tpb, (tm, 1)) >= PAD_FRONT
    pm = jnp.where(valid, p + mu_ref[...] * (p_prev - p), 0.0)

    r = pm[:, 0:512]
    k = pm[:, 512:1024]
    v = pm[:, 1024:1536]
    lora = pm[:, 1536:1536 + RW_LORA_SLAB]
    gl = pm[:, 1536 + RW_LORA_SLAB:RW_SHIFT_WIDTH]

    z = w0_ref[...] + _dot(jnp.tanh(lora).astype(BF16), w2_ref[...])
    softplus_neg = jnp.maximum(-z, 0.0) + jnp.log1p(jnp.exp(-jnp.abs(z)))
    w_log = -softplus_neg - 0.5
    lw = jnp.where(valid, -jnp.exp(w_log), 0.0)
    a = jax.nn.sigmoid(a0_ref[...] + _dot(lora.astype(BF16), a2_ref[...]))
    g = _dot(jax.nn.sigmoid(gl).astype(BF16), g2_ref[...])
    kk = k * kk_ref[...]
    nrm = jnp.sqrt(_head_sum(kk * kk, bd_ref[...]))
    kk = kk / jnp.maximum(nrm, 1e-12)
    k2 = k * (1.0 + (a - 1.0) * ka_ref[...])

    r_o[...] = r
    lw_o[...] = lw
    k_o[...] = k2
    v_o[...] = v
    kk_o[...] = kk
    a_o[...] = a
    g_o[...] = g


def _rw_prep(x, g, w, mu, w0, w2p, a0, a2p, g2, k_k, k_a, bd, lp):
    n = x.shape[0]
    tm = _pick_tile(lp, (320, 128))
    tpb = lp // tm
    vec = lambda width: _const_spec((1, width))
    row512 = pl.BlockSpec((tm, 512), lambda i: (i, 0))
    return pl.pallas_call(
        functools.partial(_rw_prep_kernel, tpb),
        grid=(n // tm,),
        in_specs=[pl.BlockSpec((tm, D_MODEL), lambda i: (i, 0)), _halo_spec(tm, ROW_HALO, D_MODEL),
                  vec(D_MODEL), _const_spec(w.shape), vec(RW_SHIFT_WIDTH), vec(512),
                  _const_spec(w2p.shape), vec(512), _const_spec(a2p.shape), _const_spec(g2.shape),
                  vec(512), vec(512), _const_spec(bd.shape)],
        out_specs=[row512] * 7,
        out_shape=[jax.ShapeDtypeStruct((n, 512), F32)] * 7,
        compiler_params=_params(),
        name="rw_prep",
    )(x, x, g, w, mu, w0, w2p, a0, a2p, g2, k_k, k_a, bd)


def _unit_lower_inverse(nmats, masks):
    eye, blk16, m32, m64 = masks
    ps = [jnp.where(blk16, n, 0.0) for n in nmats]
    ts = [eye + p for p in ps]
    for _ in range(3):
        ps = [_mm_inv(p, p) for p in ps]
        ts = [t + _mm_inv(t, p) for t, p in zip(ts, ps)]
    for m in (m32, m64):
        xs = [_mm_inv(t, jnp.where(m, n, 0.0)) for t, n in zip(ts, nmats)]
        ts = [t + _mm_inv(x, t) for t, x in zip(ts, xs)]
    return ts


def _rwkv_kernel(r_ref, lw_ref, k_ref, v_ref, kk_ref, a_ref, g_ref, lnw_ref, lnb_ref, rk_ref, bd_ref,
                 o_ref, s_ref, y_ref, m_scr, n_scr, rp_scr):
    tb = r_ref.shape[0]
    c = CHUNK
    heads = range(RW_HEADS)
    hsl = [slice(hd * RW_HEAD, (hd + 1) * RW_HEAD) for hd in heads]

    @pl.when(pl.program_id(1) == 0)
    def _():
        s_ref[...] = jnp.zeros_like(s_ref)

    ri = lax.broadcasted_iota(jnp.int32, (c, c), 0)
    ci = lax.broadcasted_iota(jnp.int32, (c, c), 1)
    strict = ri > ci
    incl = ri >= ci
    diag = ri == ci
    eye = jnp.where(diag, 1.0, 0.0).astype(F32)
    blk16 = (ri // 16) == (ci // 16)
    m32 = ((ri // 32) == (ci // 32)) & ((ri // 16) > (ci // 16))
    m64 = (ri >= 32) & (ci < 32)
    masks = (eye, blk16, m32, m64)
    ltri = jnp.where(incl, 1.0, 0.0).astype(BF16)

    def local_body(ic, carry):
        rows = pl.ds(pl.multiple_of(ic * c, c), c)
        r = r_ref[rows, :]
        lw = lw_ref[rows, :]
        k = k_ref[rows, :]
        v = v_ref[rows, :]
        kk = kk_ref[rows, :]
        a = a_ref[rows, :]
        cum = _mm2_exact_lhs(ltri, lw)
        w_t = jnp.exp(cum)
        w_prev = jnp.exp(cum - lw)
        w_inv = jnp.exp(-cum)
        w_end = w_t[c - 1:c, :]
        at = -kk * w_prev
        rt = r * w_t
        bt = kk * a * w_inv
        kt = k * w_inv
        b_end = bt * w_end
        k_end = kt * w_end

        lhs = [jnp.concatenate([at[:, sl], rt[:, sl]], axis=0) for sl in hsl]
        gb = [_mm_gram(l, bt[:, sl], NT) for l, sl in zip(lhs, hsl)]
        gk = [_mm_gram(l, kt[:, sl], NT) for l, sl in zip(lhs, hsl)]
        a_ab = [jnp.where(strict, g[:c], 0.0) for g in gb]
        a_ak = [jnp.where(strict, g[:c], 0.0) for g in gk]
        a_rb = [jnp.where(incl, g[c:], 0.0) for g in gb]
        a_rk = [jnp.where(incl, g[c:], 0.0) for g in gk]
        vh = [v[:, sl] for sl in hsl]
        tinv = _unit_lower_inverse(a_ab, masks)
        akv = [_mm_loc(x, y) for x, y in zip(a_ak, vh)]
        atp = [_mm_loc(t, at[:, sl]) for t, sl in zip(tinv, hsl)]
        uloc = [_mm_loc(t, x) for t, x in zip(tinv, akv)]
        rp = [rt[:, sl] + _mm_loc(x, y) for sl, x, y in zip(hsl, a_rb, atp)]
        yloc = [_mm_loc(x, u) + _mm_loc(z, w) for x, u, z, w in zip(a_rb, uloc, a_rk, vh)]
        mc = [jnp.where(diag, w_end[:, sl], 0.0) + _mm_loc(jnp.transpose(x), b_end[:, sl])
              for x, sl in zip(atp, hsl)]
        nc = [_mm_loc(jnp.transpose(jnp.concatenate([u, w], axis=0)),
                   jnp.concatenate([b_end[:, sl], k_end[:, sl]], axis=0))
              for u, w, sl in zip(uloc, vh, hsl)]
        for hd in heads:
            idx = ic * RW_HEADS + hd
            m_scr[idx] = mc[hd]
            n_scr[idx] = nc[hd]
            rp_scr[idx] = rp[hd]
        y_ref[rows, :] = jnp.concatenate(yloc, axis=1)
        return carry

    lax.fori_loop(0, tb // c, local_body, 0)

    def state_body(ic, carry):
        rows = pl.ds(pl.multiple_of(ic * c, c), c)
        s0 = [s_ref[hd] for hd in heads]
        ys = [_mm_state(rp_scr[ic * RW_HEADS + hd], s0[hd], NT) for hd in heads]
        sn = [_mm_state(s0[hd], m_scr[ic * RW_HEADS + hd]) + n_scr[ic * RW_HEADS + hd] for hd in heads]
        for hd in heads:
            s_ref[hd] = sn[hd]
        y_ref[rows, :] = y_ref[rows, :] + jnp.concatenate(ys, axis=1)
        return carry

    lax.fori_loop(0, tb // c, state_body, 0)

    y = y_ref[...]
    bd = bd_ref[...]
    inv_n = 1.0 / RW_HEAD
    mean = _head_sum(y, bd) * inv_n
    d = y - mean
    var = _head_sum(d * d, bd) * inv_n
    yn = d * lax.rsqrt(var + RW_GN_EPS) * lnw_ref[...] + lnb_ref[...]
    bonus = _head_sum(r_ref[...] * k_ref[...] * rk_ref[...], bd) * v_ref[...]
    o_ref[...] = ((yn + bonus) * g_ref[...]).astype(o_ref.dtype)


def _rwkv(r, lw, k2, v, kk, a, g, lnw, lnb, rk, bd, batch, lp):
    n = r.shape[0]
    tb = _pick_tile(lp, (640, 320, 128))
    nblk = lp // tb
    nmat = (tb // CHUNK) * RW_HEADS
    row = pl.BlockSpec((tb, 512), lambda b, j: (b * nblk + j, 0))
    vec = _const_spec((1, 512))
    mat = pltpu.VMEM((nmat, RW_HEAD, RW_HEAD), F32)
    return pl.pallas_call(
        _rwkv_kernel,
        grid=(batch, nblk),
        in_specs=[row] * 7 + [vec, vec, vec, _const_spec(bd.shape)],
        out_specs=row,
        out_shape=jax.ShapeDtypeStruct((n, 512), BF16),
        scratch_shapes=[pltpu.VMEM((RW_HEADS, RW_HEAD, RW_HEAD), F32), pltpu.VMEM((tb, 512), F32),
                        mat, mat, mat],
        compiler_params=_params(2, parallel=False),
        name="rwkv_chunk",
    )(r, lw, k2, v, kk, a, g, lnw, lnb, rk, bd)


def _attn_kernel(lambda_init, q_ref, k_ref, vt_ref, dl_ref, sub_ref, o_ref, qq_ref, m_ref, l_ref, acc_ref):
    t = ATTN_TILE
    heads = range(DA_HEADS)
    hsl = [slice(h * DA_VDIM, (h + 1) * DA_VDIM) for h in heads]
    qi = pl.program_id(1)
    q = q_ref[...]
    lane = lax.broadcasted_iota(jnp.int32, (t, 2 * DA_DIM), 1)
    for h in heads:
        qh = q[:, hsl[h]]
        zero = jnp.zeros_like(qh)
        qq_ref[h, 0:t, :] = jnp.where(lane < DA_DIM, qh, zero)
        qq_ref[h, t:2 * t, :] = jnp.where(lane >= DA_DIM, qh, zero)
    m_ref[...] = jnp.full_like(m_ref, MASK_VALUE)
    l_ref[...] = jnp.zeros_like(l_ref)
    acc_ref[...] = jnp.zeros_like(acc_ref)

    def step(row0, nrows, masked):
        rows = pl.ds(row0, nrows)
        if masked:
            qrow = qi * t + lax.broadcasted_iota(jnp.int32, (nrows, 2 * t), 1) % t
            krow = row0 + lax.broadcasted_iota(jnp.int32, (nrows, 2 * t), 0)
            vis = ((krow // CHUNK) <= (qrow // CHUNK)) & (krow >= PAD_FRONT)

        def scores(h):
            return _dot(k_ref[rows, hsl[h]], qq_ref[h], NT)

        s_next = scores(0)
        for h in heads:
            s = s_next
            if h + 1 < DA_HEADS:
                s_next = scores(h + 1)
            if masked:
                s = jnp.where(vis, s, MASK_VALUE)
            m_prev = m_ref[h]
            m_new = jnp.maximum(m_prev, jnp.max(s, axis=0, keepdims=True))
            alpha = jnp.exp2(m_prev - m_new)
            p = jnp.exp2(s - m_new)
            l_ref[h] = alpha * l_ref[h] + jnp.sum(p, axis=0, keepdims=True)
            acc_ref[h] = alpha * acc_ref[h] + _dot(vt_ref[hsl[h], rows], p.astype(BF16))
            m_ref[h] = m_new

    g = ATTN_KV_GROUP
    step(0, t, True)
    n_grp = jnp.maximum(qi - 1, 0) // g

    def grp_body(i, carry):
        step(pl.multiple_of(t + i * (g * t), t), g * t, False)
        return carry

    lax.fori_loop(0, n_grp, grp_body, 0)

    def rem_body(j, carry):
        step(pl.multiple_of(j * t, t), t, False)
        return carry

    lax.fori_loop(1 + n_grp * g, qi, rem_body, 0)

    @pl.when(qi > 0)
    def _():
        step(pl.multiple_of(qi * t, t), t, True)

    dl = dl_ref[...]
    lam = (jnp.exp(jnp.sum(dl[0:1] * dl[1:2], axis=1, keepdims=True))
           - jnp.exp(jnp.sum(dl[2:3] * dl[3:4], axis=1, keepdims=True)) + lambda_init)
    for h in heads:
        ot = acc_ref[h] / l_ref[h]
        ot = ot[:, 0:t] - lam * ot[:, t:2 * t]
        ms = jnp.mean(ot * ot, axis=0, keepdims=True)
        ot = ot * lax.rsqrt(ms + DA_SUBLN_EPS) * sub_ref[...] * (1.0 - lambda_init)
        o_ref[:, hsl[h]] = jnp.transpose(ot).astype(o_ref.dtype)


def _attention(q, k, vt, dl, subln_b, lambda_init, batch, lp):
    n = q.shape[0]
    t = ATTN_TILE
    nq = lp // t
    qspec = pl.BlockSpec((t, 512), lambda b, i: (b * nq + i, 0))
    kspec = pl.BlockSpec((lp, 512), lambda b, i: (b, 0))
    vspec = pl.BlockSpec((512, lp), lambda b, i: (0, b))
    return pl.pallas_call(
        functools.partial(_attn_kernel, lambda_init),
        grid=(batch, nq),
        in_specs=[qspec, kspec, vspec, _const_spec(dl.shape), _const_spec(subln_b.shape)],
        out_specs=qspec,
        out_shape=jax.ShapeDtypeStruct((n, 512), BF16),
        scratch_shapes=[pltpu.VMEM((DA_HEADS, 2 * t, 2 * DA_DIM), BF16),
                        pltpu.VMEM((DA_HEADS, 1, 2 * t), F32), pltpu.VMEM((DA_HEADS, 1, 2 * t), F32),
                        pltpu.VMEM((DA_HEADS, DA_VDIM, 2 * t), F32)],
        compiler_params=_params(2),
        name="diff_attn",
    )(q, k, vt, dl, subln_b)


def _merge_kernel(tpb, x_ref, u_ref, uh_ref, da_ref, rw_ref, gate_ref, pw_ref, ps_ref, wb_ref, wo_ref, o_ref):
    tm = x_ref.shape[0]
    row = _row_in_batch(tm, tpb, (tm, 1))
    valid = row >= PAD_FRONT
    pos1 = (row - PAD_FRONT + 1).astype(F32)
    u = u_ref[...]
    ext = jnp.concatenate([uh_ref[...], u], axis=0)
    mixed = []
    for gi, w in enumerate(POOL_WINDOWS):
        sl = slice(gi * POOL_GROUP, (gi + 1) * POOL_GROUP)
        s = ext[:, sl]
        shift = 1
        while shift < w:
            s = s + pltpu.roll(s, shift, 0)
            shift *= 2
        cnt = jnp.clip(pos1, 1.0, float(w))
        pooled = s[POOL_HALO:] / cnt - u[:, sl]
        mixed.append(_dot(pooled.astype(BF16), pw_ref[gi]))
    b_pool = jnp.concatenate(mixed, axis=1) * ps_ref[...]
    gl = gate_ref[...]
    d = D_MODEL
    merged = (jax.nn.sigmoid(gl[:, 0:d]) * _dot(b_pool.astype(BF16), wb_ref[0])
              + jax.nn.sigmoid(gl[:, d:2 * d]) * _dot(da_ref[...], wb_ref[1])
              + jax.nn.sigmoid(gl[:, 2 * d:3 * d]) * _dot(rw_ref[...], wb_ref[2]))
    xn = x_ref[...] + _dot(merged.astype(BF16), wo_ref[...])
    o_ref[...] = jnp.where(valid, xn, 0.0)


def _merge(x, u, da, rw, gates, pw, ps, wb, wo, lp):
    n = x.shape[0]
    tm = _pick_tile(lp, (320, 128))
    tpb = lp // tm
    row = lambda width: pl.BlockSpec((tm, width), lambda i: (i, 0))
    return pl.pallas_call(
        functools.partial(_merge_kernel, tpb),
        grid=(n // tm,),
        in_specs=[row(D_MODEL), row(512), _halo_spec(tm, POOL_HALO, 512), row(512), row(512),
                  row(N_BRANCH * D_MODEL), _const_spec(pw.shape), _const_spec((1, 512)),
                  _const_spec(wb.shape), _const_spec(wo.shape)],
        out_specs=row(D_MODEL),
        out_shape=jax.ShapeDtypeStruct((n, D_MODEL), F32),
        compiler_params=_params(),
        name="merge",
    )(x, u, u, da, rw, gates, pw, ps, wb, wo)


def _ffn_kernel(tpb, x_ref, xh_ref, g_ref, wu_ref, cw_ref, wd_ref, o_ref):
    tm = x_ref.shape[0]
    x = x_ref[...]
    x_ext = jnp.concatenate([xh_ref[...], x], axis=0)
    h = _rms(x_ext, g_ref[...], NORM_EPS).astype(BF16)

    def conv(col):
        u = _dot(h, wu_ref[:, col:col + FF_CHUNK])
        cw = cw_ref[:, col:col + FF_CHUNK]
        out = (u * cw[2:3] + pltpu.roll(u, 1, 0) * cw[1:2] + pltpu.roll(u, 2, 0) * cw[0:1])
        return out[ROW_HALO:]

    acc = x
    for c0 in range(0, D_FF, FF_CHUNK):
        gate = conv(c0)
        up = conv(D_FF + c0)
        act = (gate * jax.nn.sigmoid(gate) * up).astype(BF16)
        acc = acc + _dot(act, wd_ref[c0:c0 + FF_CHUNK, :])
    valid = _row_in_batch(tm, tpb, (tm, 1)) >= PAD_FRONT
    o_ref[...] = jnp.where(valid, acc, 0.0)


def _ffn(x, g, wu, cw, wd, lp):
    n = x.shape[0]
    tm = _pick_tile(lp, (320, 128))
    tpb = lp // tm
    row = pl.BlockSpec((tm, D_MODEL), lambda i: (i, 0))
    return pl.pallas_call(
        functools.partial(_ffn_kernel, tpb),
        grid=(n // tm,),
        in_specs=[row, _halo_spec(tm, ROW_HALO, D_MODEL), _const_spec((1, D_MODEL)),
                  _const_spec(wu.shape), _const_spec(cw.shape), _const_spec(wd.shape)],
        out_specs=row,
        out_shape=jax.ShapeDtypeStruct((n, D_MODEL), F32),
        compiler_params=_params(),
        name="conv_glu_ffn",
    )(x, x, g, wu, cw, wd)


def _final_kernel(x_ref, g_ref, o_ref):
    o_ref[...] = _rms(x_ref[...], g_ref[...], NORM_EPS)


def _final_norm(x, g, batch, lp, seq):
    t = ATTN_TILE
    per_in = lp // t
    per_out = seq // t
    return pl.pallas_call(
        _final_kernel,
        grid=(batch, per_out),
        in_specs=[pl.BlockSpec((t, D_MODEL), lambda b, j: (b * per_in + 1 + j, 0)),
                  _const_spec((1, D_MODEL))],
        out_specs=pl.BlockSpec((t, D_MODEL), lambda b, j: (b * per_out + j, 0)),
        out_shape=jax.ShapeDtypeStruct((batch * seq, D_MODEL), F32),
        compiler_params=_params(2),
        name="final_norm",
    )(x, g)


def _rope_tables(lp):
    half = DA_ROPE // 2
    pos = (jnp.arange(lp, dtype=jnp.int32) - PAD_FRONT).astype(F32)
    inv = ROPE_THETA ** (-jnp.arange(0, DA_ROPE, 2, dtype=F32) / DA_ROPE)
    ang = pos[:, None] * inv[None, :]
    cos, sin = jnp.cos(ang), jnp.sin(ang)
    ones = jnp.ones((lp, DA_DIM - DA_ROPE), F32)
    zeros = jnp.zeros((lp, DA_DIM - DA_ROPE), F32)
    zh = jnp.zeros((lp, half), F32)
    rc = jnp.concatenate([cos, cos, ones], axis=1)
    rs1 = jnp.concatenate([-sin, zh, zeros], axis=1)
    rs2 = jnp.concatenate([zh, sin, zeros], axis=1)
    return tuple(jnp.concatenate([t, t], axis=1) for t in (rc, rs1, rs2))


def kernel(x, meta_tokens, norm_mix, norm_ffn, norm_final, w_in, pool_w, pool_scale, da_lambda, da_subln,
           rw_mu, rw_w0, rw_w2, rw_a0, rw_a2, rw_g2, rw_k_k, rw_k_a, rw_r_k, rw_lnx_w, rw_lnx_b,
           w_branch, w_out, ffn_up, ffn_conv, ffn_down):
    batch, seq, d = x.shape
    depth = w_in.shape[0]
    assert d == D_MODEL and seq % ATTN_TILE == 0 and meta_tokens.shape == (N_META, D_MODEL)
    lp = seq + ATTN_TILE
    n = batch * lp

    meta = jnp.broadcast_to(meta_tokens[None].astype(x.dtype), (batch, N_META, d))
    xs = jnp.concatenate([jnp.zeros((batch, PAD_FRONT, d), x.dtype), meta, x], axis=1).reshape(n, d)

    rc, rs1, rs2 = _rope_tables(lp)
    hi = lax.broadcasted_iota(jnp.int32, (LANES, LANES), 0) // RW_HEAD
    hj = lax.broadcasted_iota(jnp.int32, (LANES, LANES), 1) // RW_HEAD
    bd = (hi == hj).astype(BF16)
    lora_zeros = jnp.zeros((RW_LORA_SLAB // 2, RW_WIDTH), BF16)
    row = lambda v: v.reshape(1, -1).astype(F32)

    for l in range(depth):
        lambda_init = 0.8 - 0.6 * math.exp(-0.3 * l)
        w_l = w_in[l].astype(BF16)
        g_mix = row(norm_mix[l])
        u_pool, q, k, vt = _proj_attn(xs, g_mix, w_l[:, 0:1536], w_l[:, 1536:2048].T, rc, rs1, rs2, lp)
        gates = _proj_gate(xs, g_mix, w_l[:, 2048 + RW_SHIFT_WIDTH:], lp)
        w2p = jnp.concatenate([rw_w2[l].astype(BF16), lora_zeros], axis=0)
        a2p = jnp.concatenate([lora_zeros, rw_a2[l].astype(BF16)], axis=0)
        prep = _rw_prep(xs, g_mix, w_l[:, 2048:2048 + RW_SHIFT_WIDTH], row(rw_mu[l]), row(rw_w0[l]), w2p,
                        row(rw_a0[l]), a2p, rw_g2[l].astype(BF16), row(rw_k_k[l]), row(rw_k_a[l]), bd, lp)
        b_rw = _rwkv(*prep, row(rw_lnx_w[l]), row(rw_lnx_b[l]), row(rw_r_k[l]), bd, batch, lp)
        subln_b = jnp.broadcast_to(da_subln[l].astype(F32)[:, None], (DA_VDIM, ATTN_TILE))
        b_da = _attention(q, k, vt, da_lambda[l].astype(F32), subln_b, lambda_init, batch, lp)
        xs = _merge(xs, u_pool, b_da, b_rw, gates, pool_w[l].astype(BF16), row(pool_scale[l]),
                    w_branch[l].astype(BF16), w_out[l].astype(BF16), lp)
        xs = _ffn(xs, row(norm_ffn[l]), ffn_up[l].astype(BF16), ffn_conv[l].astype(F32),
                  ffn_down[l].astype(BF16), lp)

    out = _final_norm(xs, row(norm_final), batch, lp, seq)
    return out.reshape(batch, seq, d)
```

```python
import functools
import math

import jax
import jax.numpy as jnp
from jax import lax
from jax.experimental import pallas as pl
from jax.experimental.pallas import tpu as pltpu

F32 = jnp.float32
BF16 = jnp.bfloat16

D_MODEL = 1024
N_META = 16
CHUNK = 64
ROPE_THETA = 500000.0
NORM_EPS = 1e-6
POOL_WINDOWS = (2, 4, 8, 16)
POOL_GROUP = 128
POOL_WIDTH = 512
DA_HEADS = 4
DA_DIM = 64
DA_VDIM = 128
DA_ROPE = 16
DA_SUBLN_EPS = 1e-5
RW_HEAD = 64
RW_WIDTH = 512
RW_HEADS = 8
RW_LORA_SLAB = 128
RW_GATE_LORA = 128
RW_SHIFT_WIDTH = 1792
RW_GN_EPS = 64e-5
RW_CHUNKS_PER_ITER = 2
N_BRANCH = 3
D_FF = 2816
FF_CHUNK = 1408

LANES = 128
SUBLANES = 8
VMEM_LIMIT_BYTES = 56 * 1024 * 1024

ATTN_TILE = 128
ATTN_KV_GROUP = 4
PAD_FRONT = ATTN_TILE - N_META
LOG2_E = math.log2(math.e)
MASK_VALUE = -1e30
POOL_HALO = 16
ROW_HALO = SUBLANES

NN = (((1,), (0,)), ((), ()))
NT = (((1,), (1,)), ((), ()))


def _dot(a, b, dn=NN):
    return lax.dot_general(a, b, dn, preferred_element_type=F32)


def _split(x):
    hi = x.astype(BF16)
    lo = (x - hi.astype(F32)).astype(BF16)
    return hi, lo


def _mm3(a, b, dn=NN):
    ah, al = _split(a)
    bh, bl = _split(b)
    return _dot(ah, bh, dn) + _dot(ah, bl, dn) + _dot(al, bh, dn)


def _mm1(a, b, dn=NN):
    return _dot(a.astype(BF16), b.astype(BF16), dn)


_mm_gram = _mm1
_mm_inv = _mm1
_mm_loc = _mm1
_mm_state = _mm3


def _mm2_exact_rhs(a, b_bf16):
    ah, al = _split(a)
    return _dot(ah, b_bf16) + _dot(al, b_bf16)


def _mm2_exact_lhs(a_bf16, b):
    bh, bl = _split(b)
    return _dot(a_bf16, bh) + _dot(a_bf16, bl)


def _rms(x, g, eps):
    ms = jnp.mean(x * x, axis=-1, keepdims=True)
    return x * lax.rsqrt(ms + eps) * g


def _row_in_batch(tm, tiles_per_batch, shape):
    t = pl.program_id(0) % tiles_per_batch
    return t * tm + lax.broadcasted_iota(jnp.int32, shape, 0)


def _params(n_axes=1, parallel=True):
    sem = ("parallel" if parallel else "arbitrary",) * n_axes
    return pltpu.CompilerParams(dimension_semantics=sem, vmem_limit_bytes=VMEM_LIMIT_BYTES)


def _const_spec(shape):
    nd = len(shape)
    return pl.BlockSpec(shape, lambda *_: (0,) * nd)


def _halo_spec(tm, halo, width):
    per = tm // halo
    return pl.BlockSpec((halo, width), lambda i: (jnp.maximum(i * per - 1, 0), 0))


def _pick_tile(lp, candidates):
    for c in candidates:
        if lp % c == 0:
            return c
    raise ValueError(f"no row tile for padded length {lp}")


def _proj_attn_kernel(x_ref, g_ref, w_ref, wvt_ref, rc_ref, rs1_ref, rs2_ref, u_ref, q_ref, k_ref, vt_ref):
    h = _rms(x_ref[...], g_ref[...], NORM_EPS).astype(BF16)
    p = _dot(h, w_ref[...])
    rep = POOL_WIDTH // LANES
    rc = jnp.concatenate([rc_ref[...]] * rep, axis=1)
    rs1 = jnp.concatenate([rs1_ref[...]] * rep, axis=1)
    rs2 = jnp.concatenate([rs2_ref[...]] * rep, axis=1)

    def rope(t):
        up = pltpu.roll(t, POOL_WIDTH - DA_ROPE // 2, 1)
        dn = pltpu.roll(t, DA_ROPE // 2, 1)
        return t * rc + up * rs1 + dn * rs2

    u_ref[...] = p[:, 0:512]
    q_ref[...] = (rope(p[:, 512:1024]) * (DA_DIM ** -0.5 * LOG2_E)).astype(BF16)
    k_ref[...] = rope(p[:, 1024:1536]).astype(BF16)
    vt_ref[...] = _dot(wvt_ref[...], h, NT).astype(BF16)


def _proj_attn(x, g, w, wvt, rc, rs1, rs2, lp):
    n = x.shape[0]
    tm = _pick_tile(lp, (640, 128))
    tpb = lp // tm
    row = lambda width: pl.BlockSpec((tm, width), lambda i: (i, 0))
    tab = pl.BlockSpec((tm, LANES), lambda i: (i % tpb, 0))
    return pl.pallas_call(
        _proj_attn_kernel,
        grid=(n // tm,),
        in_specs=[row(D_MODEL), _const_spec((1, D_MODEL)), _const_spec(w.shape), _const_spec(wvt.shape),
                  tab, tab, tab],
        out_specs=[row(512)] * 3 + [pl.BlockSpec((512, tm), lambda i: (0, i))],
        out_shape=[jax.ShapeDtypeStruct((n, 512), F32)] + [jax.ShapeDtypeStruct((n, 512), BF16)] * 2
        + [jax.ShapeDtypeStruct((512, n), BF16)],
        compiler_params=_params(),
        name="proj_attn",
    )(x, g, w, wvt, rc, rs1, rs2)


def _head_sum(x, bd):
    parts = [_mm2_exact_rhs(x[:, s:s + LANES], bd) for s in range(0, x.shape[1], LANES)]
    return jnp.concatenate(parts, axis=1)


def _rw_prep_kernel(tpb, x_ref, xh_ref, g_ref, w_ref, mu_ref, w0_ref, w2_ref, a0_ref, a2_ref, g2_ref,
                    kk_ref, ka_ref, bd_ref, r_o, lw_o, k_o, v_o, kk_o, a_o, g_o):
    tm = x_ref.shape[0]
    x_ext = jnp.concatenate([xh_ref[...], x_ref[...]], axis=0)
    h = _rms(x_ext, g_ref[...], NORM_EPS).astype(BF16)
    p_ext = _dot(h, w_ref[...])
    p = p_ext[ROW_HALO:]
    p_prev = pltpu.roll(p_ext, 1, 0)[ROW_HALO:]
    valid = _row_in_batch(tm, tpb, (tm, 1)) >= PAD_FRONT
    pm = jnp.where(valid, p + mu_ref[...] * (p_prev - p), 0.0)

    r = pm[:, 0:512]
    k = pm[:, 512:1024]
    v = pm[:, 1024:1536]
    lora = pm[:, 1536:1536 + RW_LORA_SLAB]
    gl = pm[:, 1536 + RW_LORA_SLAB:RW_SHIFT_WIDTH]

    z = w0_ref[...] + _dot(jnp.tanh(lora).astype(BF16), w2_ref[...])
    softplus_neg = jnp.maximum(-z, 0.0) + jnp.log1p(jnp.exp(-jnp.abs(z)))
    w_log = -softplus_neg - 0.5
    lw = jnp.where(valid, -jnp.exp(w_log), 0.0)
    a = jax.nn.sigmoid(a0_ref[...] + _dot(lora.astype(BF16), a2_ref[...]))
    g = _dot(jax.nn.sigmoid(gl).astype(BF16), g2_ref[...])
    kk = k * kk_ref[...]
    nrm = jnp.sqrt(_head_sum(kk * kk, bd_ref[...]))
    kk = kk / jnp.maximum(nrm, 1e-12)
    k2 = k * (1.0 + (a - 1.0) * ka_ref[...])

    r_o[...] = r.astype(BF16)
    lw_o[...] = lw
    k_o[...] = k2.astype(BF16)
    v_o[...] = v.astype(BF16)
    kk_o[...] = kk.astype(BF16)
    a_o[...] = a.astype(BF16)
    g_o[...] = g.astype(BF16)


def _rw_prep(x, g, w, mu, w0, w2p, a0, a2p, g2, k_k, k_a, bd, lp):
    n = x.shape[0]
    tm = _pick_tile(lp, (320, 128))
    tpb = lp // tm
    vec = lambda width: _const_spec((1, width))
    row512 = pl.BlockSpec((tm, 512), lambda i: (i, 0))
    return pl.pallas_call(
        functools.partial(_rw_prep_kernel, tpb),
        grid=(n // tm,),
        in_specs=[pl.BlockSpec((tm, D_MODEL), lambda i: (i, 0)), _halo_spec(tm, ROW_HALO, D_MODEL),
                  vec(D_MODEL), _const_spec(w.shape), vec(RW_SHIFT_WIDTH), vec(512),
                  _const_spec(w2p.shape), vec(512), _const_spec(a2p.shape), _const_spec(g2.shape),
                  vec(512), vec(512), _const_spec(bd.shape)],
        out_specs=[row512] * 7,
        out_shape=[jax.ShapeDtypeStruct((n, 512), F32 if i == 1 else BF16) for i in range(7)],
        compiler_params=_params(),
        name="rw_prep",
    )(x, x, g, w, mu, w0, w2p, a0, a2p, g2, k_k, k_a, bd)


def _unit_lower_inverse(nmats, masks):
    eye, blk16, m32, m64 = masks
    ps = [jnp.where(blk16, n, 0.0) for n in nmats]
    ts = [eye + p for p in ps]
    for _ in range(3):
        ps = [_mm_inv(p, p) for p in ps]
        ts = [t + _mm_inv(t, p) for t, p in zip(ts, ps)]
    for m in (m32, m64):
        xs = [_mm_inv(t, jnp.where(m, n, 0.0)) for t, n in zip(ts, nmats)]
        ts = [t + _mm_inv(x, t) for t, x in zip(ts, xs)]
    return ts


def _rwkv_kernel(r_ref, lw_ref, k_ref, v_ref, kk_ref, a_ref, g_ref, lnw_ref, lnb_ref, rk_ref, bd_ref,
                 o_ref, s_ref, y_ref, m_scr, n_scr, rp_scr):
    tb = r_ref.shape[0]
    c = CHUNK
    heads = range(RW_HEADS)
    hsl = [slice(hd * RW_HEAD, (hd + 1) * RW_HEAD) for hd in heads]

    @pl.when(pl.program_id(1) == 0)
    def _():
        s_ref[...] = jnp.zeros_like(s_ref)

    ri = lax.broadcasted_iota(jnp.int32, (c, c), 0)
    ci = lax.broadcasted_iota(jnp.int32, (c, c), 1)
    strict = ri > ci
    incl = ri >= ci
    diag = ri == ci
    eye = jnp.where(diag, 1.0, 0.0).astype(F32)
    blk16 = (ri // 16) == (ci // 16)
    m32 = ((ri // 32) == (ci // 32)) & ((ri // 16) > (ci // 16))
    m64 = (ri >= 32) & (ci < 32)
    masks = (eye, blk16, m32, m64)
    ltri = jnp.where(incl, 1.0, 0.0).astype(BF16)

    def local_body(ip, carry):
        at, rt, bt, kt, b_end, k_end, w_end, vh, rows = [], [], [], [], [], [], [], [], []
        for j in range(RW_CHUNKS_PER_ITER):
            rw = pl.ds(pl.multiple_of((ip * RW_CHUNKS_PER_ITER + j) * c, c), c)
            lw = lw_ref[rw, :]
            kk = kk_ref[rw, :].astype(F32)
            cum = _mm2_exact_lhs(ltri, lw)
            w_t = jnp.exp(cum)
            w_inv = jnp.exp(-cum)
            we = w_t[c - 1:c, :]
            a_t = -kk * jnp.exp(cum - lw)
            r_t = r_ref[rw, :].astype(F32) * w_t
            b_t = kk * a_ref[rw, :].astype(F32) * w_inv
            k_t = k_ref[rw, :].astype(F32) * w_inv
            v = v_ref[rw, :].astype(F32)
            for sl in hsl:
                at.append(a_t[:, sl])
                rt.append(r_t[:, sl])
                bt.append(b_t[:, sl])
                kt.append(k_t[:, sl])
                b_end.append(b_t[:, sl] * we[:, sl])
                k_end.append(k_t[:, sl] * we[:, sl])
                w_end.append(we[:, sl])
                vh.append(v[:, sl])
            rows.append(rw)

        lhs = [jnp.concatenate([x, y], axis=0) for x, y in zip(at, rt)]
        gb = [_mm_gram(l, x, NT) for l, x in zip(lhs, bt)]
        gk = [_mm_gram(l, x, NT) for l, x in zip(lhs, kt)]
        a_ab = [jnp.where(strict, g[:c], 0.0) for g in gb]
        a_ak = [jnp.where(strict, g[:c], 0.0) for g in gk]
        a_rb = [jnp.where(incl, g[c:], 0.0) for g in gb]
        a_rk = [jnp.where(incl, g[c:], 0.0) for g in gk]
        tinv = _unit_lower_inverse(a_ab, masks)
        akv = [_mm_loc(x, y) for x, y in zip(a_ak, vh)]
        atp = [_mm_loc(t, x) for t, x in zip(tinv, at)]
        uloc = [_mm_loc(t, x) for t, x in zip(tinv, akv)]
        rp = [x + _mm_loc(y, z) for x, y, z in zip(rt, a_rb, atp)]
        yloc = [_mm_loc(x, u) + _mm_loc(z, w) for x, u, z, w in zip(a_rb, uloc, a_rk, vh)]
        mc = [jnp.where(diag, w, 0.0) + _mm_loc(jnp.transpose(x), b)
              for w, x, b in zip(w_end, atp, b_end)]
        nc = [_mm_loc(jnp.transpose(jnp.concatenate([u, w], axis=0)), jnp.concatenate([b, k], axis=0))
              for u, w, b, k in zip(uloc, vh, b_end, k_end)]
        for j in range(RW_CHUNKS_PER_ITER):
            for hd in heads:
                src = j * RW_HEADS + hd
                idx = (ip * RW_CHUNKS_PER_ITER + j) * RW_HEADS + hd
                m_scr[idx] = mc[src]
                n_scr[idx] = nc[src]
                rp_scr[idx] = rp[src]
            y_ref[rows[j], :] = jnp.concatenate(yloc[j * RW_HEADS:(j + 1) * RW_HEADS], axis=1)
        return carry

    lax.fori_loop(0, tb // (c * RW_CHUNKS_PER_ITER), local_body, 0)

    def state_body(ic, carry):
        rows = pl.ds(pl.multiple_of(ic * c, c), c)
        s0 = [s_ref[hd] for hd in heads]
        ys = [_mm_state(rp_scr[ic * RW_HEADS + hd], s0[hd], NT) for hd in heads]
        sn = [_mm_state(s0[hd], m_scr[ic * RW_HEADS + hd]) + n_scr[ic * RW_HEADS + hd] for hd in heads]
        for hd in heads:
            s_ref[hd] = sn[hd]
        y_ref[rows, :] = y_ref[rows, :] + jnp.concatenate(ys, axis=1)
        return carry

    lax.fori_loop(0, tb // c, state_body, 0)

    y = y_ref[...]
    bd = bd_ref[...]
    inv_n = 1.0 / RW_HEAD
    mean = _head_sum(y, bd) * inv_n
    d = y - mean
    var = _head_sum(d * d, bd) * inv_n
    yn = d * lax.rsqrt(var + RW_GN_EPS) * lnw_ref[...] + lnb_ref[...]
    rk = r_ref[...].astype(F32) * k_ref[...].astype(F32) * rk_ref[...]
    bonus = _head_sum(rk, bd) * v_ref[...].astype(F32)
    o_ref[...] = ((yn + bonus) * g_ref[...].astype(F32)).astype(o_ref.dtype)


def _rwkv(r, lw, k2, v, kk, a, g, lnw, lnb, rk, bd, batch, lp):
    n = r.shape[0]
    tb = _pick_tile(lp, (640, 128))
    nblk = lp // tb
    nmat = (tb // CHUNK) * RW_HEADS
    row = pl.BlockSpec((tb, 512), lambda b, j: (b * nblk + j, 0))
    vec = _const_spec((1, 512))
    mat = pltpu.VMEM((nmat, RW_HEAD, RW_HEAD), F32)
    return pl.pallas_call(
        _rwkv_kernel,
        grid=(batch, nblk),
        in_specs=[row] * 7 + [vec, vec, vec, _const_spec(bd.shape)],
        out_specs=row,
        out_shape=jax.ShapeDtypeStruct((n, 512), BF16),
        scratch_shapes=[pltpu.VMEM((RW_HEADS, RW_HEAD, RW_HEAD), F32), pltpu.VMEM((tb, 512), F32),
                        mat, mat, mat],
        compiler_params=_params(2, parallel=False),
        name="rwkv_chunk",
    )(r, lw, k2, v, kk, a, g, lnw, lnb, rk, bd)


def _attn_kernel(lambda_init, q_ref, k_ref, vt_ref, dl_ref, sub_ref, o_ref, qq_ref, m_ref, l_ref, acc_ref,
                 sa_ref, sb_ref):
    t = ATTN_TILE
    heads = range(DA_HEADS)
    hsl = [slice(h * DA_VDIM, (h + 1) * DA_VDIM) for h in heads]
    qi = pl.program_id(1)
    q = q_ref[...]
    lane = lax.broadcasted_iota(jnp.int32, (t, 2 * DA_DIM), 1)
    for h in heads:
        qh = q[:, hsl[h]]
        zero = jnp.zeros_like(qh)
        qq_ref[h, 0:t, :] = jnp.where(lane < DA_DIM, qh, zero)
        qq_ref[h, t:2 * t, :] = jnp.where(lane >= DA_DIM, qh, zero)
    m_ref[...] = jnp.full_like(m_ref, MASK_VALUE)
    l_ref[...] = jnp.zeros_like(l_ref)
    acc_ref[...] = jnp.zeros_like(acc_ref)

    def scores(h, row0, nrows):
        return _dot(k_ref[pl.ds(row0, nrows), hsl[h]], qq_ref[h], NT)

    def update(h, s, row0, nrows):
        m_prev = m_ref[h]
        m_new = jnp.maximum(m_prev, jnp.max(s, axis=0, keepdims=True))
        alpha = jnp.exp2(m_prev - m_new)
        p = jnp.exp2(s - m_new)
        l_ref[h] = alpha * l_ref[h] + jnp.sum(p, axis=0, keepdims=True)
        pv = _dot(vt_ref[hsl[h], pl.ds(row0, nrows)], p.astype(BF16))
        acc_ref[h] = alpha * acc_ref[h] + pv
        m_ref[h] = m_new

    def tile_step(row0, masked):
        if masked:
            qrow = qi * t + lax.broadcasted_iota(jnp.int32, (t, 2 * t), 1) % t
            krow = row0 + lax.broadcasted_iota(jnp.int32, (t, 2 * t), 0)
            vis = ((krow // CHUNK) <= (qrow // CHUNK)) & (krow >= PAD_FRONT)
        s_next = scores(0, row0, t)
        for h in heads:
            s = s_next
            if h + 1 < DA_HEADS:
                s_next = scores(h + 1, row0, t)
            update(h, jnp.where(vis, s, MASK_VALUE) if masked else s, row0, t)

    g = ATTN_KV_GROUP
    gt = g * t
    tile_step(0, True)
    n_grp = jnp.maximum(qi - 1, 0) // g
    grp_row = lambda i: pl.multiple_of(t + i * gt, t)

    def fill(buf, i):
        for h in heads:
            buf[h] = scores(h, grp_row(i), gt)

    def drain(buf, i):
        for h in heads:
            update(h, buf[h], grp_row(i), gt)

    @pl.when(n_grp > 0)
    def _():
        fill(sa_ref, 0)

        def pair_body(i, carry):
            fill(sb_ref, 2 * i + 1)
            drain(sa_ref, 2 * i)
            fill(sa_ref, jnp.minimum(2 * i + 2, n_grp - 1))
            drain(sb_ref, 2 * i + 1)
            return carry

        lax.fori_loop(0, n_grp // 2, pair_body, 0)

        @pl.when(n_grp % 2 == 1)
        def _():
            drain(sa_ref, n_grp - 1)

    def rem_body(j, carry):
        tile_step(pl.multiple_of(j * t, t), False)
        return carry

    lax.fori_loop(1 + n_grp * g, qi, rem_body, 0)

    @pl.when(qi > 0)
    def _():
        tile_step(pl.multiple_of(qi * t, t), True)

    dl = dl_ref[...]
    lam = (jnp.exp(jnp.sum(dl[0:1] * dl[1:2], axis=1, keepdims=True))
           - jnp.exp(jnp.sum(dl[2:3] * dl[3:4], axis=1, keepdims=True)) + lambda_init)
    for h in heads:
        ot = acc_ref[h] / l_ref[h]
        ot = ot[:, 0:t] - lam * ot[:, t:2 * t]
        ms = jnp.mean(ot * ot, axis=0, keepdims=True)
        ot = ot * lax.rsqrt(ms + DA_SUBLN_EPS) * sub_ref[...] * (1.0 - lambda_init)
        o_ref[:, hsl[h]] = jnp.transpose(ot).astype(o_ref.dtype)


def _attention(q, k, vt, dl, subln_b, lambda_init, batch, lp):
    n = q.shape[0]
    t = ATTN_TILE
    nq = lp // t
    qspec = pl.BlockSpec((t, 512), lambda b, i: (b * nq + i, 0))
    kspec = pl.BlockSpec((lp, 512), lambda b, i: (b, 0))
    vspec = pl.BlockSpec((512, lp), lambda b, i: (0, b))
    return pl.pallas_call(
        functools.partial(_attn_kernel, lambda_init),
        grid=(batch, nq),
        in_specs=[qspec, kspec, vspec, _const_spec(dl.shape), _const_spec(subln_b.shape)],
        out_specs=qspec,
        out_shape=jax.ShapeDtypeStruct((n, 512), BF16),
        scratch_shapes=[pltpu.VMEM((DA_HEADS, 2 * t, 2 * DA_DIM), BF16),
                        pltpu.VMEM((DA_HEADS, 1, 2 * t), F32), pltpu.VMEM((DA_HEADS, 1, 2 * t), F32),
                        pltpu.VMEM((DA_HEADS, DA_VDIM, 2 * t), F32),
                        pltpu.VMEM((DA_HEADS, ATTN_KV_GROUP * t, 2 * t), F32),
                        pltpu.VMEM((DA_HEADS, ATTN_KV_GROUP * t, 2 * t), F32)],
        compiler_params=_params(2),
        name="diff_attn",
    )(q, k, vt, dl, subln_b)


def _merge_kernel(tpb, x_ref, g_ref, u_ref, uh_ref, da_ref, rw_ref, wg_ref, pw_ref, ps_ref, wb_ref, wo_ref, o_ref):
    tm = x_ref.shape[0]
    row = _row_in_batch(tm, tpb, (tm, 1))
    valid = row >= PAD_FRONT
    pos1 = (row - PAD_FRONT + 1).astype(F32)
    u = u_ref[...]
    ext = jnp.concatenate([uh_ref[...], u], axis=0)
    mixed = []
    for gi, w in enumerate(POOL_WINDOWS):
        sl = slice(gi * POOL_GROUP, (gi + 1) * POOL_GROUP)
        s = ext[:, sl]
        shift = 1
        while shift < w:
            s = s + pltpu.roll(s, shift, 0)
            shift *= 2
        cnt = jnp.clip(pos1, 1.0, float(w))
        pooled = s[POOL_HALO:] / cnt - u[:, sl]
        mixed.append(_dot(pooled.astype(BF16), pw_ref[gi]))
    b_pool = jnp.concatenate(mixed, axis=1) * ps_ref[...]
    x = x_ref[...]
    gl = _dot(_rms(x, g_ref[...], NORM_EPS).astype(BF16), wg_ref[...])
    d = D_MODEL
    merged = (jax.nn.sigmoid(gl[:, 0:d]) * _dot(b_pool.astype(BF16), wb_ref[0])
              + jax.nn.sigmoid(gl[:, d:2 * d]) * _dot(da_ref[...], wb_ref[1])
              + jax.nn.sigmoid(gl[:, 2 * d:3 * d]) * _dot(rw_ref[...], wb_ref[2]))
    xn = x + _dot(merged.astype(BF16), wo_ref[...])
    o_ref[...] = jnp.where(valid, xn, 0.0)


def _merge(x, g, u, da, rw, wg, pw, ps, wb, wo, lp):
    n = x.shape[0]
    tm = _pick_tile(lp, (320, 128))
    tpb = lp // tm
    row = lambda width: pl.BlockSpec((tm, width), lambda i: (i, 0))
    return pl.pallas_call(
        functools.partial(_merge_kernel, tpb),
        grid=(n // tm,),
        in_specs=[row(D_MODEL), _const_spec((1, D_MODEL)), row(512), _halo_spec(tm, POOL_HALO, 512), row(512),
                  row(512), _const_spec(wg.shape), _const_spec(pw.shape), _const_spec((1, 512)),
                  _const_spec(wb.shape), _const_spec(wo.shape)],
        out_specs=row(D_MODEL),
        out_shape=jax.ShapeDtypeStruct((n, D_MODEL), F32),
        compiler_params=_params(),
        name="merge",
    )(x, g, u, u, da, rw, wg, pw, ps, wb, wo)


def _ffn_kernel(tpb, x_ref, xh_ref, g_ref, wu_ref, cw_ref, wd_ref, o_ref):
    tm = x_ref.shape[0]
    x = x_ref[...]
    x_ext = jnp.concatenate([xh_ref[...], x], axis=0)
    h = _rms(x_ext, g_ref[...], NORM_EPS).astype(BF16)

    def conv(col):
        u = _dot(h, wu_ref[:, col:col + FF_CHUNK])
        cw = cw_ref[:, col:col + FF_CHUNK]
        out = (u * cw[2:3] + pltpu.roll(u, 1, 0) * cw[1:2] + pltpu.roll(u, 2, 0) * cw[0:1])
        return out[ROW_HALO:]

    acc = x
    for c0 in range(0, D_FF, FF_CHUNK):
        gate = conv(c0)
        up = conv(D_FF + c0)
        act = (gate * jax.nn.sigmoid(gate) * up).astype(BF16)
        acc = acc + _dot(act, wd_ref[c0:c0 + FF_CHUNK, :])
    valid = _row_in_batch(tm, tpb, (tm, 1)) >= PAD_FRONT
    o_ref[...] = jnp.where(valid, acc, 0.0)


def _ffn(x, g, wu, cw, wd, lp):
    n = x.shape[0]
    tm = _pick_tile(lp, (320, 128))
    tpb = lp // tm
    row = pl.BlockSpec((tm, D_MODEL), lambda i: (i, 0))
    return pl.pallas_call(
        functools.partial(_ffn_kernel, tpb),
        grid=(n // tm,),
        in_specs=[row, _halo_spec(tm, ROW_HALO, D_MODEL), _const_spec((1, D_MODEL)),
                  _const_spec(wu.shape), _const_spec(cw.shape), _const_spec(wd.shape)],
        out_specs=row,
        out_shape=jax.ShapeDtypeStruct((n, D_MODEL), F32),
        compiler_params=_params(),
        name="conv_glu_ffn",
    )(x, x, g, wu, cw, wd)


def _final_kernel(x_ref, g_ref, o_ref):
    o_ref[...] = _rms(x_ref[...], g_ref[...], NORM_EPS)


def _final_norm(x, g, batch, lp, seq):
    t = ATTN_TILE
    per_in = lp // t
    per_out = seq // t
    return pl.pallas_call(
        _final_kernel,
        grid=(batch, per_out),
        in_specs=[pl.BlockSpec((t, D_MODEL), lambda b, j: (b * per_in + 1 + j, 0)),
                  _const_spec((1, D_MODEL))],
        out_specs=pl.BlockSpec((t, D_MODEL), lambda b, j: (b * per_out + j, 0)),
        out_shape=jax.ShapeDtypeStruct((batch * seq, D_MODEL), F32),
        compiler_params=_params(2),
        name="final_norm",
    )(x, g)


def _rope_tables(lp):
    half = DA_ROPE // 2
    pos = (jnp.arange(lp, dtype=jnp.int32) - PAD_FRONT).astype(F32)
    inv = ROPE_THETA ** (-jnp.arange(0, DA_ROPE, 2, dtype=F32) / DA_ROPE)
    ang = pos[:, None] * inv[None, :]
    cos, sin = jnp.cos(ang), jnp.sin(ang)
    ones = jnp.ones((lp, DA_DIM - DA_ROPE), F32)
    zeros = jnp.zeros((lp, DA_DIM - DA_ROPE), F32)
    zh = jnp.zeros((lp, half), F32)
    rc = jnp.concatenate([cos, cos, ones], axis=1)
    rs1 = jnp.concatenate([-sin, zh, zeros], axis=1)
    rs2 = jnp.concatenate([zh, sin, zeros], axis=1)
    return tuple(jnp.concatenate([t, t], axis=1) for t in (rc, rs1, rs2))


def kernel(x, meta_tokens, norm_mix, norm_ffn, norm_final, w_in, pool_w, pool_scale, da_lambda, da_subln,
           rw_mu, rw_w0, rw_w2, rw_a0, rw_a2, rw_g2, rw_k_k, rw_k_a, rw_r_k, rw_lnx_w, rw_lnx_b,
           w_branch, w_out, ffn_up, ffn_conv, ffn_down):
    batch, seq, d = x.shape
    depth = w_in.shape[0]
    assert d == D_MODEL and seq % ATTN_TILE == 0 and meta_tokens.shape == (N_META, D_MODEL)
    lp = seq + ATTN_TILE
    n = batch * lp

    meta = jnp.broadcast_to(meta_tokens[None].astype(x.dtype), (batch, N_META, d))
    xs = jnp.concatenate([jnp.zeros((batch, PAD_FRONT, d), x.dtype), meta, x], axis=1).reshape(n, d)

    rc, rs1, rs2 = _rope_tables(lp)
    hi = lax.broadcasted_iota(jnp.int32, (LANES, LANES), 0) // RW_HEAD
    hj = lax.broadcasted_iota(jnp.int32, (LANES, LANES), 1) // RW_HEAD
    bd = (hi == hj).astype(BF16)
    lora_zeros = jnp.zeros((RW_LORA_SLAB // 2, RW_WIDTH), BF16)
    row = lambda v: v.reshape(1, -1).astype(F32)

    for l in range(depth):
        lambda_init = 0.8 - 0.6 * math.exp(-0.3 * l)
        w_l = w_in[l].astype(BF16)
        g_mix = row(norm_mix[l])
        u_pool, q, k, vt = _proj_attn(xs, g_mix, w_l[:, 0:1536], w_l[:, 1536:2048].T, rc, rs1, rs2, lp)
        w2p = jnp.concatenate([rw_w2[l].astype(BF16), lora_zeros], axis=0)
        a2p = jnp.concatenate([lora_zeros, rw_a2[l].astype(BF16)], axis=0)
        prep = _rw_prep(xs, g_mix, w_l[:, 2048:2048 + RW_SHIFT_WIDTH], row(rw_mu[l]), row(rw_w0[l]), w2p,
                        row(rw_a0[l]), a2p, rw_g2[l].astype(BF16), row(rw_k_k[l]), row(rw_k_a[l]), bd, lp)
        b_rw = _rwkv(*prep, row(rw_lnx_w[l]), row(rw_lnx_b[l]), row(rw_r_k[l]), bd, batch, lp)
        subln_b = jnp.broadcast_to(da_subln[l].astype(F32)[:, None], (DA_VDIM, ATTN_TILE))
        b_da = _attention(q, k, vt, da_lambda[l].astype(F32), subln_b, lambda_init, batch, lp)
        xs = _merge(xs, g_mix, u_pool, b_da, b_rw, w_l[:, 2048 + RW_SHIFT_WIDTH:], pool_w[l].astype(BF16), row(pool_scale[l]),
                    w_branch[l].astype(BF16), w_out[l].astype(BF16), lp)
        xs = _ffn(xs, row(norm_ffn[l]), ffn_up[l].astype(BF16), ffn_conv[l].astype(F32),
                  ffn_down[l].astype(BF16), lp)

    out = _final_norm(xs, row(norm_final), batch, lp, seq)
    return out.reshape(batch, seq, d)
```

```python
import functools
import math

import jax
import jax.numpy as jnp
from jax import lax
from jax.experimental import pallas as pl
from jax.experimental.pallas import tpu as pltpu

F32 = jnp.float32
BF16 = jnp.bfloat16

D_MODEL = 1024
N_META = 16
CHUNK = 64
ROPE_THETA = 500000.0
NORM_EPS = 1e-6
POOL_WINDOWS = (2, 4, 8, 16)
POOL_GROUP = 128
POOL_WIDTH = 512
DA_HEADS = 4
DA_DIM = 64
DA_VDIM = 128
DA_ROPE = 16
DA_SUBLN_EPS = 1e-5
RW_HEAD = 64
RW_WIDTH = 512
RW_HEADS = 8
RW_LORA_SLAB = 128
RW_GATE_LORA = 128
RW_SHIFT_WIDTH = 1792
RW_GN_EPS = 64e-5
RW_CHUNKS_PER_ITER = (5, 2, 1)
N_BRANCH = 3
D_FF = 2816
FF_CHUNK = 1408

LANES = 128
SUBLANES = 8
VMEM_LIMIT_BYTES = 56 * 1024 * 1024

ATTN_TILE = 128
ATTN_KV_GROUP = 4
ATTN_ANCHOR_LAG = 2
PAD_FRONT = ATTN_TILE - N_META
LOG2_E = math.log2(math.e)
MASK_VALUE = -1e30
POOL_HALO = 16
ROW_HALO = SUBLANES

NN = (((1,), (0,)), ((), ()))
NT = (((1,), (1,)), ((), ()))


def _dot(a, b, dn=NN):
    return lax.dot_general(a, b, dn, preferred_element_type=F32)


def _split(x):
    hi = x.astype(BF16)
    lo = (x - hi.astype(F32)).astype(BF16)
    return hi, lo


def _mm1(a, b, dn=NN):
    return _dot(a.astype(BF16), b.astype(BF16), dn)


def _mm2_exact_rhs(a, b_bf16):
    ah, al = _split(a)
    return _dot(ah, b_bf16) + _dot(al, b_bf16)


def _mm2_exact_lhs(a_bf16, b):
    bh, bl = _split(b)
    return _dot(a_bf16, bh) + _dot(a_bf16, bl)


def _rms(x, g, eps):
    ms = jnp.mean(x * x, axis=-1, keepdims=True)
    return x * lax.rsqrt(ms + eps) * g


def _row_in_batch(tm, tiles_per_batch, shape):
    t = pl.program_id(0) % tiles_per_batch
    return t * tm + lax.broadcasted_iota(jnp.int32, shape, 0)


def _params(n_axes=1, parallel=True):
    sem = ("parallel" if parallel else "arbitrary",) * n_axes
    return pltpu.CompilerParams(dimension_semantics=sem, vmem_limit_bytes=VMEM_LIMIT_BYTES)


def _const_spec(shape):
    nd = len(shape)
    return pl.BlockSpec(shape, lambda *_: (0,) * nd)


def _halo_spec(tm, halo, width):
    per = tm // halo
    return pl.BlockSpec((halo, width), lambda i: (jnp.maximum(i * per - 1, 0), 0))


def _pick_tile(lp, candidates):
    for c in candidates:
        if lp % c == 0:
            return c
    raise ValueError(f"no row tile for padded length {lp}")


def _proj_attn_kernel(x_ref, g_ref, w_ref, wvt_ref, rc_ref, rs1_ref, rs2_ref, u_ref, q_ref, k_ref, vt_ref):
    h = _rms(x_ref[...], g_ref[...], NORM_EPS).astype(BF16)
    p = _dot(h, w_ref[...])
    rep = POOL_WIDTH // LANES
    rc = jnp.concatenate([rc_ref[...]] * rep, axis=1)
    rs1 = jnp.concatenate([rs1_ref[...]] * rep, axis=1)
    rs2 = jnp.concatenate([rs2_ref[...]] * rep, axis=1)

    def rope(t):
        up = pltpu.roll(t, POOL_WIDTH - DA_ROPE // 2, 1)
        dn = pltpu.roll(t, DA_ROPE // 2, 1)
        return t * rc + up * rs1 + dn * rs2

    u_ref[...] = p[:, 0:512]
    q_ref[...] = (rope(p[:, 512:1024]) * (DA_DIM ** -0.5 * LOG2_E)).astype(BF16)
    k_ref[...] = rope(p[:, 1024:1536]).astype(BF16)
    vt_ref[...] = _dot(wvt_ref[...], h, NT).astype(BF16)


def _proj_attn(x, g, w, wvt, rc, rs1, rs2, lp):
    n = x.shape[0]
    tm = _pick_tile(lp, (640, 128))
    tpb = lp // tm
    row = lambda width: pl.BlockSpec((tm, width), lambda i: (i, 0))
    tab = pl.BlockSpec((tm, LANES), lambda i: (i % tpb, 0))
    return pl.pallas_call(
        _proj_attn_kernel,
        grid=(n // tm,),
        in_specs=[row(D_MODEL), _const_spec((1, D_MODEL)), _const_spec(w.shape), _const_spec(wvt.shape),
                  tab, tab, tab],
        out_specs=[row(512)] * 3 + [pl.BlockSpec((512, tm), lambda i: (0, i))],
        out_shape=[jax.ShapeDtypeStruct((n, 512), F32)] + [jax.ShapeDtypeStruct((n, 512), BF16)] * 2
        + [jax.ShapeDtypeStruct((512, n), BF16)],
        compiler_params=_params(),
        name="proj_attn",
    )(x, g, w, wvt, rc, rs1, rs2)


def _head_sum(x, bd):
    parts = [_mm2_exact_rhs(x[:, s:s + LANES], bd) for s in range(0, x.shape[1], LANES)]
    return jnp.concatenate(parts, axis=1)


def _rw_prep_kernel(tpb, x_ref, xh_ref, g_ref, w_ref, mu_ref, w0_ref, w2_ref, a0_ref, a2_ref, g2_ref,
                    kk_ref, ka_ref, bd_ref, r_o, lw_o, k_o, v_o, kk_o, a_o, g_o):
    tm = x_ref.shape[0]
    x_ext = jnp.concatenate([xh_ref[...], x_ref[...]], axis=0)
    h = _rms(x_ext, g_ref[...], NORM_EPS).astype(BF16)
    p_ext = _dot(h, w_ref[...])
    p = p_ext[ROW_HALO:]
    p_prev = pltpu.roll(p_ext, 1, 0)[ROW_HALO:]
    valid = _row_in_batch(tm, tpb, (tm, 1)) >= PAD_FRONT
    pm = jnp.where(valid, p + mu_ref[...] * (p_prev - p), 0.0)

    r = pm[:, 0:512]
    k = pm[:, 512:1024]
    v = pm[:, 1024:1536]
    lora = pm[:, 1536:1536 + RW_LORA_SLAB]
    gl = pm[:, 1536 + RW_LORA_SLAB:RW_SHIFT_WIDTH]

    z = w0_ref[...] + _dot(jnp.tanh(lora).astype(BF16), w2_ref[...])
    softplus_neg = jnp.maximum(-z, 0.0) + jnp.log1p(jnp.exp(-jnp.abs(z)))
    w_log = -softplus_neg - 0.5
    lw = jnp.where(valid, -jnp.exp(w_log), 0.0)
    a = jax.nn.sigmoid(a0_ref[...] + _dot(lora.astype(BF16), a2_ref[...]))
    g = _dot(jax.nn.sigmoid(gl).astype(BF16), g2_ref[...])
    kk = k * kk_ref[...]
    nrm = jnp.sqrt(_head_sum(kk * kk, bd_ref[...]))
    kk = kk / jnp.maximum(nrm, 1e-12)
    k2 = k * (1.0 + (a - 1.0) * ka_ref[...])

    r_o[...] = r.astype(BF16)
    lw_o[...] = lw
    k_o[...] = k2.astype(BF16)
    v_o[...] = v.astype(BF16)
    kk_o[...] = kk.astype(BF16)
    a_o[...] = a.astype(BF16)
    g_o[...] = g.astype(BF16)


def _rw_prep(x, g, w, mu, w0, w2p, a0, a2p, g2, k_k, k_a, bd, lp):
    n = x.shape[0]
    tm = _pick_tile(lp, (320, 128))
    tpb = lp // tm
    vec = lambda width: _const_spec((1, width))
    row512 = pl.BlockSpec((tm, 512), lambda i: (i, 0))
    return pl.pallas_call(
        functools.partial(_rw_prep_kernel, tpb),
        grid=(n // tm,),
        in_specs=[pl.BlockSpec((tm, D_MODEL), lambda i: (i, 0)), _halo_spec(tm, ROW_HALO, D_MODEL),
                  vec(D_MODEL), _const_spec(w.shape), vec(RW_SHIFT_WIDTH), vec(512),
                  _const_spec(w2p.shape), vec(512), _const_spec(a2p.shape), _const_spec(g2.shape),
                  vec(512), vec(512), _const_spec(bd.shape)],
        out_specs=[row512] * 7,
        out_shape=[jax.ShapeDtypeStruct((n, 512), F32 if i == 1 else BF16) for i in range(7)],
        compiler_params=_params(),
        name="rw_prep",
    )(x, x, g, w, mu, w0, w2p, a0, a2p, g2, k_k, k_a, bd)


def _unit_lower_inverse(nmats, masks):
    eye, blk16, m32, m64 = masks
    ps = [jnp.where(blk16, n, 0.0) for n in nmats]
    ts = [eye + p for p in ps]
    for _ in range(3):
        ps = [_mm1(p, p) for p in ps]
        ts = [t + _mm1(t, p) for t, p in zip(ts, ps)]
    for m in (m32, m64):
        xs = [_mm1(t, jnp.where(m, n, 0.0)) for t, n in zip(ts, nmats)]
        ts = [t + _mm1(x, t) for t, x in zip(ts, xs)]
    return ts


def _rwkv_kernel(r_ref, lw_ref, k_ref, v_ref, kk_ref, a_ref, g_ref, lnw_ref, lnb_ref, rk_ref, bd_ref,
                 o_ref, s_ref, y_ref, m_scr, n_scr, rp_scr):
    tb = r_ref.shape[0]
    c = CHUNK
    heads = range(RW_HEADS)
    hsl = [slice(hd * RW_HEAD, (hd + 1) * RW_HEAD) for hd in heads]

    @pl.when(pl.program_id(1) == 0)
    def _():
        s_ref[...] = jnp.zeros_like(s_ref)

    ri = lax.broadcasted_iota(jnp.int32, (c, c), 0)
    ci = lax.broadcasted_iota(jnp.int32, (c, c), 1)
    strict = ri > ci
    incl = ri >= ci
    diag = ri == ci
    eye = jnp.where(diag, 1.0, 0.0).astype(F32)
    blk16 = (ri // 16) == (ci // 16)
    m32 = ((ri // 32) == (ci // 32)) & ((ri // 16) > (ci // 16))
    m64 = (ri >= 32) & (ci < 32)
    masks = (eye, blk16, m32, m64)
    ltri = jnp.where(incl, 1.0, 0.0).astype(BF16)

    cpi = next(d for d in RW_CHUNKS_PER_ITER if (tb // c) % d == 0)

    def local_body(ip, carry):
        at, rt, bt, kt, b_end, k_end, w_end, vh, rows = [], [], [], [], [], [], [], [], []
        for j in range(cpi):
            rw = pl.ds(pl.multiple_of((ip * cpi + j) * c, c), c)
            lw = lw_ref[rw, :]
            kk = kk_ref[rw, :].astype(F32)
            cum = _mm2_exact_lhs(ltri, lw)
            w_t = jnp.exp(cum)
            w_inv = jnp.exp(-cum)
            we = w_t[c - 1:c, :]
            a_t = -kk * jnp.exp(cum - lw)
            r_t = r_ref[rw, :].astype(F32) * w_t
            b_t = kk * a_ref[rw, :].astype(F32) * w_inv
            k_t = k_ref[rw, :].astype(F32) * w_inv
            v = v_ref[rw, :].astype(F32)
            for sl in hsl:
                at.append(a_t[:, sl])
                rt.append(r_t[:, sl])
                bt.append(b_t[:, sl])
                kt.append(k_t[:, sl])
                b_end.append(b_t[:, sl] * we[:, sl])
                k_end.append(k_t[:, sl] * we[:, sl])
                w_end.append(we[:, sl])
                vh.append(v[:, sl])
            rows.append(rw)

        lhs = [jnp.concatenate([x, y], axis=0) for x, y in zip(at, rt)]
        gb = [_mm1(l, x, NT) for l, x in zip(lhs, bt)]
        gk = [_mm1(l, x, NT) for l, x in zip(lhs, kt)]
        a_ab = [jnp.where(strict, g[:c], 0.0) for g in gb]
        a_ak = [jnp.where(strict, g[:c], 0.0) for g in gk]
        a_rb = [jnp.where(incl, g[c:], 0.0) for g in gb]
        a_rk = [jnp.where(incl, g[c:], 0.0) for g in gk]
        tinv = _unit_lower_inverse(a_ab, masks)
        akv = [_mm1(x, y) for x, y in zip(a_ak, vh)]
        atp = [_mm1(t, x) for t, x in zip(tinv, at)]
        uloc = [_mm1(t, x) for t, x in zip(tinv, akv)]
        rp = [x + _mm1(y, z) for x, y, z in zip(rt, a_rb, atp)]
        yloc = [_mm1(x, u) + _mm1(z, w) for x, u, z, w in zip(a_rb, uloc, a_rk, vh)]
        mc = [jnp.where(diag, w, 0.0) + _mm1(jnp.transpose(x), b)
              for w, x, b in zip(w_end, atp, b_end)]
        nc = [_mm1(jnp.transpose(jnp.concatenate([u, w], axis=0)), jnp.concatenate([b, k], axis=0))
              for u, w, b, k in zip(uloc, vh, b_end, k_end)]
        for j in range(cpi):
            for hd in heads:
                src = j * RW_HEADS + hd
                idx = (ip * cpi + j) * RW_HEADS + hd
                m_scr[idx] = mc[src]
                n_scr[idx] = nc[src]
                rp_scr[idx] = rp[src]
            y_ref[rows[j], :] = jnp.concatenate(yloc[j * RW_HEADS:(j + 1) * RW_HEADS], axis=1)
        return carry

    lax.fori_loop(0, tb // (c * cpi), local_body, 0)

    def state_body(ic, carry):
        rows = pl.ds(pl.multiple_of(ic * c, c), c)
        s0 = [s_ref[hd] for hd in heads]
        ys = [_mm1(rp_scr[ic * RW_HEADS + hd], s0[hd], NT) for hd in heads]
        sn = [_mm1(s0[hd], m_scr[ic * RW_HEADS + hd]) + n_scr[ic * RW_HEADS + hd] for hd in heads]
        for hd in heads:
            s_ref[hd] = sn[hd]
        y_ref[rows, :] = y_ref[rows, :] + jnp.concatenate(ys, axis=1)
        return carry

    lax.fori_loop(0, tb // c, state_body, 0)

    y = y_ref[...]
    bd = bd_ref[...]
    inv_n = 1.0 / RW_HEAD
    mean = _head_sum(y, bd) * inv_n
    d = y - mean
    var = _head_sum(d * d, bd) * inv_n
    yn = d * lax.rsqrt(var + RW_GN_EPS) * lnw_ref[...] + lnb_ref[...]
    rk = r_ref[...].astype(F32) * k_ref[...].astype(F32) * rk_ref[...]
    bonus = _head_sum(rk, bd) * v_ref[...].astype(F32)
    o_ref[...] = ((yn + bonus) * g_ref[...].astype(F32)).astype(o_ref.dtype)


def _rwkv(r, lw, k2, v, kk, a, g, lnw, lnb, rk, bd, batch, lp):
    n = r.shape[0]
    tb = _pick_tile(lp, (640, 128))
    nblk = lp // tb
    nmat = (tb // CHUNK) * RW_HEADS
    row = pl.BlockSpec((tb, 512), lambda b, j: (b * nblk + j, 0))
    vec = _const_spec((1, 512))
    mat = pltpu.VMEM((nmat, RW_HEAD, RW_HEAD), F32)
    return pl.pallas_call(
        _rwkv_kernel,
        grid=(batch, nblk),
        in_specs=[row] * 7 + [vec, vec, vec, _const_spec(bd.shape)],
        out_specs=row,
        out_shape=jax.ShapeDtypeStruct((n, 512), BF16),
        scratch_shapes=[pltpu.VMEM((RW_HEADS, RW_HEAD, RW_HEAD), F32), pltpu.VMEM((tb, 512), F32),
                        mat, mat, mat],
        compiler_params=_params(2, parallel=False),
        name="rwkv_chunk",
    )(r, lw, k2, v, kk, a, g, lnw, lnb, rk, bd)


def _attn_kernel(lambda_init, nb, q_ref, k_ref, vt_ref, dl_ref, sub_ref, o_ref, qq_ref, m_ref, l_ref, acc_ref,
                 sa_ref, sb_ref):
    t = ATTN_TILE
    g = ATTN_KV_GROUP
    gt = g * t
    lp = k_ref.shape[1]
    chains = [(bb, slice(h * DA_VDIM, (h + 1) * DA_VDIM)) for bb in range(nb) for h in range(DA_HEADS)]
    ids = range(len(chains))
    qi = pl.program_id(1)
    lane = lax.broadcasted_iota(jnp.int32, (t, 2 * DA_DIM), 1)
    for c, (bb, hs) in enumerate(chains):
        qh = q_ref[bb, :, hs]
        zero = jnp.zeros_like(qh)
        qq_ref[c, 0:t, :] = jnp.where(lane < DA_DIM, qh, zero)
        qq_ref[c, t:2 * t, :] = jnp.where(lane >= DA_DIM, qh, zero)
    m_ref[...] = jnp.full_like(m_ref, MASK_VALUE)
    l_ref[...] = jnp.zeros_like(l_ref)
    acc_ref[...] = jnp.zeros_like(acc_ref)

    def scores(c, row0, nrows):
        bb, hs = chains[c]
        return _dot(k_ref[bb, pl.ds(row0, nrows), hs], qq_ref[c], NT)

    def values(c, row0, nrows):
        bb, hs = chains[c]
        return vt_ref[hs, pl.ds(pl.multiple_of(bb * lp + row0, t), nrows)]

    def update(c, s, pv, anchor=None):
        m_prev = m_ref[c]
        if anchor is not None:
            bits = pltpu.bitcast(anchor, jnp.uint32)
            m_prev = m_prev + pltpu.bitcast((bits >> 16) >> 16, F32)
        m_new = jnp.maximum(m_prev, jnp.max(s, axis=0, keepdims=True))
        alpha = jnp.exp2(m_prev - m_new)
        p = jnp.exp2(s - m_new)
        l_ref[c] = alpha * l_ref[c] + jnp.sum(p, axis=0, keepdims=True)
        acc_ref[c] = alpha * acc_ref[c] + pv(p.astype(BF16))
        m_ref[c] = m_new

    n_grp = jnp.maximum(qi - 1, 0) // g

    blk_row = pl.multiple_of(jnp.maximum(qi - (g - 1), 0) * t, t)
    qrow0 = qi * t + lax.broadcasted_iota(jnp.int32, (t, 2 * t), 1) % t
    krow0 = lax.broadcasted_iota(jnp.int32, (t, 2 * t), 0)
    vis0 = ((krow0 // CHUNK) <= (qrow0 // CHUNK)) & (krow0 >= PAD_FRONT)
    qrow1 = qi * t + lax.broadcasted_iota(jnp.int32, (gt, 2 * t), 1) % t
    krow1 = blk_row + lax.broadcasted_iota(jnp.int32, (gt, 2 * t), 0)
    vis1 = ((krow1 // CHUNK) <= (qrow1 // CHUNK)) & (krow1 >= (1 + n_grp * g) * t)
    for c in ids:
        s = jnp.concatenate([jnp.where(vis0, scores(c, 0, t), MASK_VALUE),
                             jnp.where(vis1, scores(c, blk_row, gt), MASK_VALUE)], axis=0)
        update(c, s, lambda p, c=c: _dot(values(c, 0, t), p[0:t]) + _dot(values(c, blk_row, gt), p[t:]))

    grp_row = lambda i: pl.multiple_of(t + i * gt, t)

    def fill(buf, i):
        for c in ids:
            buf[c] = scores(c, grp_row(i), gt)

    def drain(buf, i):
        for c in ids:
            update(c, buf[c], lambda p, c=c: _dot(values(c, grp_row(i), gt), p))

    @pl.when(n_grp > 0)
    def _():
        fill(sa_ref, 0)

        def pair_body(i, carry):
            fill(sb_ref, 2 * i + 1)
            drain(sa_ref, 2 * i)
            nxt = jnp.minimum(2 * i + 2, n_grp - 1)
            anchors = [None] * ATTN_ANCHOR_LAG
            for c in ids:
                s_new = scores(c, grp_row(nxt), gt)
                sa_ref[c] = s_new
                update(c, sb_ref[c], lambda p, c=c: _dot(values(c, grp_row(2 * i + 1), gt), p), anchors[0])
                anchors = anchors[1:] + [s_new[0:1, :]]
            return carry

        lax.fori_loop(0, n_grp // 2, pair_body, 0)

        @pl.when(n_grp % 2 == 1)
        def _():
            drain(sa_ref, n_grp - 1)

    dl = dl_ref[...]
    lam = (jnp.exp(jnp.sum(dl[0:1] * dl[1:2], axis=1, keepdims=True))
           - jnp.exp(jnp.sum(dl[2:3] * dl[3:4], axis=1, keepdims=True)) + lambda_init)
    for c, (bb, hs) in enumerate(chains):
        ot = acc_ref[c] / l_ref[c]
        ot = ot[:, 0:t] - lam * ot[:, t:2 * t]
        ms = jnp.mean(ot * ot, axis=0, keepdims=True)
        ot = ot * lax.rsqrt(ms + DA_SUBLN_EPS) * sub_ref[...] * (1.0 - lambda_init)
        o_ref[bb, :, hs] = jnp.transpose(ot).astype(o_ref.dtype)


def _attention(q, k, vt, dl, subln_b, lambda_init, batch, lp):
    t = ATTN_TILE
    assert lp >= t + ATTN_KV_GROUP * t
    nb = 2 if batch % 2 == 0 else 1
    nc = nb * DA_HEADS
    once = pl.Buffered(1)
    qspec = pl.BlockSpec((nb, t, 512), lambda b, i: (b, i, 0))
    kspec = pl.BlockSpec((nb, lp, 512), lambda b, i: (b, 0, 0), pipeline_mode=once)
    vspec = pl.BlockSpec((512, nb * lp), lambda b, i: (0, b), pipeline_mode=once)
    scr = pltpu.VMEM((nc, ATTN_KV_GROUP * t, 2 * t), F32)
    out = pl.pallas_call(
        functools.partial(_attn_kernel, lambda_init, nb),
        grid=(batch // nb, lp // t),
        in_specs=[qspec, kspec, vspec, _const_spec(dl.shape), _const_spec(subln_b.shape)],
        out_specs=qspec,
        out_shape=jax.ShapeDtypeStruct((batch, lp, 512), BF16),
        scratch_shapes=[pltpu.VMEM((nc, 2 * t, 2 * DA_DIM), BF16),
                        pltpu.VMEM((nc, 1, 2 * t), F32), pltpu.VMEM((nc, 1, 2 * t), F32),
                        pltpu.VMEM((nc, DA_VDIM, 2 * t), F32), scr, scr],
        compiler_params=_params(2),
        name="diff_attn",
    )(q.reshape(batch, lp, 512), k.reshape(batch, lp, 512), vt, dl, subln_b)
    return out.reshape(batch * lp, 512)


def _merge_kernel(tpb, x_ref, g_ref, u_ref, uh_ref, da_ref, rw_ref, wg_ref, pw_ref, ps_ref, wb_ref, wo_ref, o_ref):
    tm = x_ref.shape[0]
    row = _row_in_batch(tm, tpb, (tm, 1))
    valid = row >= PAD_FRONT
    pos1 = (row - PAD_FRONT + 1).astype(F32)
    u = u_ref[...]
    ext = jnp.concatenate([uh_ref[...], u], axis=0)
    mixed = []
    for gi, w in enumerate(POOL_WINDOWS):
        sl = slice(gi * POOL_GROUP, (gi + 1) * POOL_GROUP)
        s = ext[:, sl]
        shift = 1
        while shift < w:
            s = s + pltpu.roll(s, shift, 0)
            shift *= 2
        cnt = jnp.clip(pos1, 1.0, float(w))
        pooled = s[POOL_HALO:] / cnt - u[:, sl]
        mixed.append(_dot(pooled.astype(BF16), pw_ref[gi]))
    b_pool = jnp.concatenate(mixed, axis=1) * ps_ref[...]
    x = x_ref[...]
    gl = _dot(_rms(x, g_ref[...], NORM_EPS).astype(BF16), wg_ref[...])
    d = D_MODEL
    merged = (jax.nn.sigmoid(gl[:, 0:d]) * _dot(b_pool.astype(BF16), wb_ref[0])
              + jax.nn.sigmoid(gl[:, d:2 * d]) * _dot(da_ref[...], wb_ref[1])
              + jax.nn.sigmoid(gl[:, 2 * d:3 * d]) * _dot(rw_ref[...], wb_ref[2]))
    xn = x + _dot(merged.astype(BF16), wo_ref[...])
    o_ref[...] = jnp.where(valid, xn, 0.0)


def _merge(x, g, u, da, rw, wg, pw, ps, wb, wo, lp):
    n = x.shape[0]
    tm = _pick_tile(lp, (320, 128))
    tpb = lp // tm
    row = lambda width: pl.BlockSpec((tm, width), lambda i: (i, 0))
    return pl.pallas_call(
        functools.partial(_merge_kernel, tpb),
        grid=(n // tm,),
        in_specs=[row(D_MODEL), _const_spec((1, D_MODEL)), row(512), _halo_spec(tm, POOL_HALO, 512), row(512),
                  row(512), _const_spec(wg.shape), _const_spec(pw.shape), _const_spec((1, 512)),
                  _const_spec(wb.shape), _const_spec(wo.shape)],
        out_specs=row(D_MODEL),
        out_shape=jax.ShapeDtypeStruct((n, D_MODEL), F32),
        compiler_params=_params(),
        name="merge",
    )(x, g, u, u, da, rw, wg, pw, ps, wb, wo)


def _ffn_kernel(tpb, x_ref, xh_ref, g_ref, wu_ref, cw_ref, wd_ref, *rest):
    gf_ref, o_ref = rest if len(rest) == 2 else (None, rest[0])
    tm = x_ref.shape[0]
    x = x_ref[...]
    x_ext = jnp.concatenate([xh_ref[...], x], axis=0)
    h = _rms(x_ext, g_ref[...], NORM_EPS).astype(BF16)

    def conv(col):
        u = _dot(h, wu_ref[:, col:col + FF_CHUNK])
        cw = cw_ref[:, col:col + FF_CHUNK]
        out = (u * cw[2:3] + pltpu.roll(u, 1, 0) * cw[1:2] + pltpu.roll(u, 2, 0) * cw[0:1])
        return out[ROW_HALO:]

    acc = x
    for c0 in range(0, D_FF, FF_CHUNK):
        gate = conv(c0)
        up = conv(D_FF + c0)
        act = (gate * jax.nn.sigmoid(gate) * up).astype(BF16)
        acc = acc + _dot(act, wd_ref[c0:c0 + FF_CHUNK, :])
    valid = _row_in_batch(tm, tpb, (tm, 1)) >= PAD_FRONT
    res = jnp.where(valid, acc, 0.0)
    o_ref[...] = res if gf_ref is None else _rms(res, gf_ref[...], NORM_EPS)


def _ffn(x, g, wu, cw, wd, lp, g_final=None):
    n = x.shape[0]
    tm = _pick_tile(lp, (320, 128))
    tpb = lp // tm
    row = pl.BlockSpec((tm, D_MODEL), lambda i: (i, 0))
    ins = [x, x, g, wu, cw, wd]
    specs = [row, _halo_spec(tm, ROW_HALO, D_MODEL), _const_spec((1, D_MODEL)),
             _const_spec(wu.shape), _const_spec(cw.shape), _const_spec(wd.shape)]
    if g_final is not None:
        ins.append(g_final)
        specs.append(_const_spec((1, D_MODEL)))
    return pl.pallas_call(
        functools.partial(_ffn_kernel, tpb),
        grid=(n // tm,),
        in_specs=specs,
        out_specs=row,
        out_shape=jax.ShapeDtypeStruct((n, D_MODEL), F32),
        compiler_params=_params(),
        name="conv_glu_ffn",
    )(*ins)


def _rope_tables(lp):
    half = DA_ROPE // 2
    pos = (jnp.arange(lp, dtype=jnp.int32) - PAD_FRONT).astype(F32)
    inv = ROPE_THETA ** (-jnp.arange(0, DA_ROPE, 2, dtype=F32) / DA_ROPE)
    ang = pos[:, None] * inv[None, :]
    cos, sin = jnp.cos(ang), jnp.sin(ang)
    ones = jnp.ones((lp, DA_DIM - DA_ROPE), F32)
    zeros = jnp.zeros((lp, DA_DIM - DA_ROPE), F32)
    zh = jnp.zeros((lp, half), F32)
    rc = jnp.concatenate([cos, cos, ones], axis=1)
    rs1 = jnp.concatenate([-sin, zh, zeros], axis=1)
    rs2 = jnp.concatenate([zh, sin, zeros], axis=1)
    return tuple(jnp.concatenate([t, t], axis=1) for t in (rc, rs1, rs2))


def kernel(x, meta_tokens, norm_mix, norm_ffn, norm_final, w_in, pool_w, pool_scale, da_lambda, da_subln,
           rw_mu, rw_w0, rw_w2, rw_a0, rw_a2, rw_g2, rw_k_k, rw_k_a, rw_r_k, rw_lnx_w, rw_lnx_b,
           w_branch, w_out, ffn_up, ffn_conv, ffn_down):
    batch, seq, d = x.shape
    depth = w_in.shape[0]
    assert d == D_MODEL and seq % ATTN_TILE == 0 and meta_tokens.shape == (N_META, D_MODEL)
    lp = seq + ATTN_TILE
    n = batch * lp

    meta = jnp.broadcast_to(meta_tokens[None].astype(x.dtype), (batch, N_META, d))
    xs = jnp.concatenate([jnp.zeros((batch, PAD_FRONT, d), x.dtype), meta, x], axis=1).reshape(n, d)

    rc, rs1, rs2 = _rope_tables(lp)
    hi = lax.broadcasted_iota(jnp.int32, (LANES, LANES), 0) // RW_HEAD
    hj = lax.broadcasted_iota(jnp.int32, (LANES, LANES), 1) // RW_HEAD
    bd = (hi == hj).astype(BF16)
    lora_zeros = jnp.zeros((RW_LORA_SLAB // 2, RW_WIDTH), BF16)
    row = lambda v: v.reshape(1, -1).astype(F32)

    for l in range(depth):
        lambda_init = 0.8 - 0.6 * math.exp(-0.3 * l)
        w_l = w_in[l].astype(BF16)
        g_mix = row(norm_mix[l])
        u_pool, q, k, vt = _proj_attn(xs, g_mix, w_l[:, 0:1536], w_l[:, 1536:2048].T, rc, rs1, rs2, lp)
        w2p = jnp.concatenate([rw_w2[l].astype(BF16), lora_zeros], axis=0)
        a2p = jnp.concatenate([lora_zeros, rw_a2[l].astype(BF16)], axis=0)
        prep = _rw_prep(xs, g_mix, w_l[:, 2048:2048 + RW_SHIFT_WIDTH], row(rw_mu[l]), row(rw_w0[l]), w2p,
                        row(rw_a0[l]), a2p, rw_g2[l].astype(BF16), row(rw_k_k[l]), row(rw_k_a[l]), bd, lp)
        b_rw = _rwkv(*prep, row(rw_lnx_w[l]), row(rw_lnx_b[l]), row(rw_r_k[l]), bd, batch, lp)
        subln_b = jnp.broadcast_to(da_subln[l].astype(F32)[:, None], (DA_VDIM, ATTN_TILE))
        b_da = _attention(q, k, vt, da_lambda[l].astype(F32), subln_b, lambda_init, batch, lp)
        xs = _merge(xs, g_mix, u_pool, b_da, b_rw, w_l[:, 2048 + RW_SHIFT_WIDTH:], pool_w[l].astype(BF16), row(pool_scale[l]),
                    w_branch[l].astype(BF16), w_out[l].astype(BF16), lp)
        xs = _ffn(xs, row(norm_ffn[l]), ffn_up[l].astype(BF16), ffn_conv[l].astype(F32),
                  ffn_down[l].astype(BF16), lp, g_final=row(norm_final) if l == depth - 1 else None)

    return xs.reshape(batch, lp, d)[:, ATTN_TILE:]
```

```python
import functools
import math

import jax
import jax.numpy as jnp
from jax import lax
from jax.experimental import pallas as pl
from jax.experimental.pallas import tpu as pltpu

F32 = jnp.float32
BF16 = jnp.bfloat16

D_MODEL = 1024
N_META = 16
CHUNK = 64
ROPE_THETA = 500000.0
NORM_EPS = 1e-6
POOL_WINDOWS = (2, 4, 8, 16)
POOL_GROUP = 128
POOL_WIDTH = 512
DA_HEADS = 4
DA_DIM = 64
DA_VDIM = 128
DA_ROPE = 16
DA_SUBLN_EPS = 1e-5
RW_HEAD = 64
RW_WIDTH = 512
RW_HEADS = 8
RW_LORA_SLAB = 128
RW_GATE_LORA = 128
RW_SHIFT_WIDTH = 1792
RW_GN_EPS = 64e-5
RW_CHUNKS_PER_ITER = (5, 2, 1)
N_BRANCH = 3
D_FF = 2816
FF_CHUNKS = ((0, 1536), (1536, 1280))

LANES = 128
SUBLANES = 8
VMEM_LIMIT_BYTES = 56 * 1024 * 1024

ATTN_TILE = 128
ATTN_KV_GROUP = 4
ATTN_ANCHOR_LAG = 2
PAD_FRONT = ATTN_TILE - N_META
LOG2_E = math.log2(math.e)
MASK_VALUE = -1e30
POOL_HALO = 16
ROW_HALO = SUBLANES

NN = (((1,), (0,)), ((), ()))
NT = (((1,), (1,)), ((), ()))


def _dot(a, b, dn=NN):
    return lax.dot_general(a, b, dn, preferred_element_type=F32)


def _split(x):
    hi = x.astype(BF16)
    lo = (x - hi.astype(F32)).astype(BF16)
    return hi, lo


def _mm1(a, b, dn=NN):
    return _dot(a.astype(BF16), b.astype(BF16), dn)


def _mm2_exact_rhs(a, b_bf16):
    ah, al = _split(a)
    return _dot(ah, b_bf16) + _dot(al, b_bf16)


def _mm2_exact_lhs(a_bf16, b):
    bh, bl = _split(b)
    return _dot(a_bf16, bh) + _dot(a_bf16, bl)


def _rms(x, g, eps):
    ms = jnp.mean(x * x, axis=-1, keepdims=True)
    return x * lax.rsqrt(ms + eps) * g


def _row_in_batch(tm, tiles_per_batch, shape):
    t = pl.program_id(0) % tiles_per_batch
    return t * tm + lax.broadcasted_iota(jnp.int32, shape, 0)


def _params(n_axes=1, parallel=True):
    sem = ("parallel" if parallel else "arbitrary",) * n_axes
    return pltpu.CompilerParams(dimension_semantics=sem, vmem_limit_bytes=VMEM_LIMIT_BYTES)


def _const_spec(shape):
    nd = len(shape)
    return pl.BlockSpec(shape, lambda *_: (0,) * nd)


def _weight_spec(shape):
    nd = len(shape)
    return pl.BlockSpec(shape, lambda *_: (0,) * nd, pipeline_mode=pl.Buffered(1))


def _halo_spec(tm, halo, width):
    per = tm // halo
    return pl.BlockSpec((halo, width), lambda i: (jnp.maximum(i * per - 1, 0), 0))


def _pick_tile(lp, candidates):
    for c in candidates:
        if lp % c == 0:
            return c
    raise ValueError(f"no row tile for padded length {lp}")


def _proj_attn_kernel(x_ref, g_ref, w_ref, wvt_ref, rc_ref, rs1_ref, rs2_ref, u_ref, q_ref, k_ref, vt_ref):
    h = _rms(x_ref[...], g_ref[...], NORM_EPS).astype(BF16)
    p = _dot(h, w_ref[...])
    rep = POOL_WIDTH // LANES
    rc = jnp.concatenate([rc_ref[...]] * rep, axis=1)
    rs1 = jnp.concatenate([rs1_ref[...]] * rep, axis=1)
    rs2 = jnp.concatenate([rs2_ref[...]] * rep, axis=1)

    def rope(t):
        up = pltpu.roll(t, POOL_WIDTH - DA_ROPE // 2, 1)
        dn = pltpu.roll(t, DA_ROPE // 2, 1)
        return t * rc + up * rs1 + dn * rs2

    u_ref[...] = p[:, 0:512]
    q_ref[...] = (rope(p[:, 512:1024]) * (DA_DIM ** -0.5 * LOG2_E)).astype(BF16)
    k_ref[...] = rope(p[:, 1024:1536]).astype(BF16)
    vt_ref[...] = _dot(wvt_ref[...], h, NT).astype(BF16)


def _proj_attn(x, g, w, wvt, rc, rs1, rs2, lp):
    n = x.shape[0]
    tm = _pick_tile(lp, (640, 128))
    tpb = lp // tm
    row = lambda width: pl.BlockSpec((tm, width), lambda i: (i, 0))
    tab = pl.BlockSpec((tm, LANES), lambda i: (i % tpb, 0))
    return pl.pallas_call(
        _proj_attn_kernel,
        grid=(n // tm,),
        in_specs=[row(D_MODEL), _const_spec((1, D_MODEL)), _weight_spec(w.shape), _weight_spec(wvt.shape),
                  tab, tab, tab],
        out_specs=[row(512)] * 3 + [pl.BlockSpec((512, tm), lambda i: (0, i))],
        out_shape=[jax.ShapeDtypeStruct((n, 512), F32)] + [jax.ShapeDtypeStruct((n, 512), BF16)] * 2
        + [jax.ShapeDtypeStruct((512, n), BF16)],
        compiler_params=_params(),
        name="proj_attn",
    )(x, g, w, wvt, rc, rs1, rs2)


def _head_sum(x, bd, passes=2):
    one = (lambda v: _dot(v.astype(BF16), bd)) if passes == 1 else (lambda v: _mm2_exact_rhs(v, bd))
    return jnp.concatenate([one(x[:, s:s + LANES]) for s in range(0, x.shape[1], LANES)], axis=1)


def _rw_prep_kernel(tpb, x_ref, xh_ref, g_ref, w_ref, mu_ref, w0_ref, w2_ref, a0_ref, a2_ref, g2_ref,
                    kk_ref, ka_ref, bd_ref, r_o, lw_o, k_o, v_o, kk_o, a_o, g_o):
    tm = x_ref.shape[0]
    x_ext = jnp.concatenate([xh_ref[...], x_ref[...]], axis=0)
    h = _rms(x_ext, g_ref[...], NORM_EPS).astype(BF16)
    p_ext = _dot(h, w_ref[...])
    p = p_ext[ROW_HALO:]
    p_prev = pltpu.roll(p_ext, 1, 0)[ROW_HALO:]
    valid = _row_in_batch(tm, tpb, (tm, 1)) >= PAD_FRONT
    pm = jnp.where(valid, p + mu_ref[...] * (p_prev - p), 0.0)

    r = pm[:, 0:512]
    k = pm[:, 512:1024]
    v = pm[:, 1024:1536]
    lora = pm[:, 1536:1536 + RW_LORA_SLAB]
    gl = pm[:, 1536 + RW_LORA_SLAB:RW_SHIFT_WIDTH]

    z = w0_ref[...] + _dot(jnp.tanh(lora).astype(BF16), w2_ref[...])
    softplus_neg = jnp.maximum(-z, 0.0) + jnp.log1p(jnp.exp(-jnp.abs(z)))
    w_log = -softplus_neg - 0.5
    lw = jnp.where(valid, -jnp.exp(w_log), 0.0)
    a = jax.nn.sigmoid(a0_ref[...] + _dot(lora.astype(BF16), a2_ref[...]))
    g = _dot(jax.nn.sigmoid(gl).astype(BF16), g2_ref[...])
    kk = k * kk_ref[...]
    nrm = jnp.sqrt(_head_sum(kk * kk, bd_ref[...], passes=1))
    kk = kk / jnp.maximum(nrm, 1e-12)
    k2 = k * (1.0 + (a - 1.0) * ka_ref[...])

    r_o[...] = r.astype(BF16)
    lw_o[...] = lw
    k_o[...] = k2.astype(BF16)
    v_o[...] = v.astype(BF16)
    kk_o[...] = kk.astype(BF16)
    a_o[...] = a.astype(BF16)
    g_o[...] = g.astype(BF16)


def _rw_prep(x, g, w, mu, w0, w2p, a0, a2p, g2, k_k, k_a, bd, lp):
    n = x.shape[0]
    tm = _pick_tile(lp, (640, 320, 128))
    tpb = lp // tm
    vec = lambda width: _const_spec((1, width))
    row512 = pl.BlockSpec((tm, 512), lambda i: (i, 0))
    return pl.pallas_call(
        functools.partial(_rw_prep_kernel, tpb),
        grid=(n // tm,),
        in_specs=[pl.BlockSpec((tm, D_MODEL), lambda i: (i, 0)), _halo_spec(tm, ROW_HALO, D_MODEL),
                  vec(D_MODEL), _weight_spec(w.shape), vec(RW_SHIFT_WIDTH), vec(512),
                  _const_spec(w2p.shape), vec(512), _const_spec(a2p.shape), _const_spec(g2.shape),
                  vec(512), vec(512), _const_spec(bd.shape)],
        out_specs=[row512] * 7,
        out_shape=[jax.ShapeDtypeStruct((n, 512), F32 if i == 1 else BF16) for i in range(7)],
        compiler_params=_params(),
        name="rw_prep",
    )(x, x, g, w, mu, w0, w2p, a0, a2p, g2, k_k, k_a, bd)


def _unit_lower_inverse(nmats, masks):
    eye, blk16, m32, m64 = masks
    ps = [jnp.where(blk16, n, 0.0) for n in nmats]
    ts = [eye + p for p in ps]
    for _ in range(3):
        ps = [_mm1(p, p) for p in ps]
        ts = [t + _mm1(t, p) for t, p in zip(ts, ps)]
    for m in (m32, m64):
        xs = [_mm1(t, jnp.where(m, n, 0.0)) for t, n in zip(ts, nmats)]
        ts = [t + _mm1(x, t) for t, x in zip(ts, xs)]
    return ts


def _rwkv_kernel(r_ref, lw_ref, k_ref, v_ref, kk_ref, a_ref, g_ref, lnw_ref, lnb_ref, rk_ref, bd_ref,
                 o_ref, s_ref, y_ref, m_scr, n_scr, rp_scr):
    tb = r_ref.shape[0]
    c = CHUNK
    heads = range(RW_HEADS)
    hsl = [slice(hd * RW_HEAD, (hd + 1) * RW_HEAD) for hd in heads]

    @pl.when(pl.program_id(1) == 0)
    def _():
        s_ref[...] = jnp.zeros_like(s_ref)

    ri = lax.broadcasted_iota(jnp.int32, (c, c), 0)
    ci = lax.broadcasted_iota(jnp.int32, (c, c), 1)
    strict = ri > ci
    incl = ri >= ci
    diag = ri == ci
    eye = jnp.where(diag, 1.0, 0.0).astype(F32)
    blk16 = (ri // 16) == (ci // 16)
    m32 = ((ri // 32) == (ci // 32)) & ((ri // 16) > (ci // 16))
    m64 = (ri >= 32) & (ci < 32)
    masks = (eye, blk16, m32, m64)
    ltri = jnp.where(incl, 1.0, 0.0).astype(BF16)

    cpi = next(d for d in RW_CHUNKS_PER_ITER if (tb // c) % d == 0)

    def local_body(ip, carry):
        at, rt, bt, kt, b_end, k_end, w_end, vh, rows = [], [], [], [], [], [], [], [], []
        for j in range(cpi):
            rw = pl.ds(pl.multiple_of((ip * cpi + j) * c, c), c)
            lw = lw_ref[rw, :]
            kk = kk_ref[rw, :].astype(F32)
            cum = _mm2_exact_lhs(ltri, lw)
            w_t = jnp.exp(cum)
            w_inv = jnp.exp(-cum)
            we = w_t[c - 1:c, :]
            a_t = -kk * jnp.exp(cum - lw)
            r_t = r_ref[rw, :].astype(F32) * w_t
            b_t = kk * a_ref[rw, :].astype(F32) * w_inv
            k_t = k_ref[rw, :].astype(F32) * w_inv
            v = v_ref[rw, :].astype(F32)
            for sl in hsl:
                at.append(a_t[:, sl])
                rt.append(r_t[:, sl])
                bt.append(b_t[:, sl])
                kt.append(k_t[:, sl])
                b_end.append(b_t[:, sl] * we[:, sl])
                k_end.append(k_t[:, sl] * we[:, sl])
                w_end.append(we[:, sl])
                vh.append(v[:, sl])
            rows.append(rw)

        lhs = [jnp.concatenate([x, y], axis=0) for x, y in zip(at, rt)]
        gb = [_mm1(l, x, NT) for l, x in zip(lhs, bt)]
        gk = [_mm1(l, x, NT) for l, x in zip(lhs, kt)]
        a_ab = [jnp.where(strict, g[:c], 0.0) for g in gb]
        a_ak = [jnp.where(strict, g[:c], 0.0) for g in gk]
        a_rb = [jnp.where(incl, g[c:], 0.0) for g in gb]
        a_rk = [jnp.where(incl, g[c:], 0.0) for g in gk]
        tinv = _unit_lower_inverse(a_ab, masks)
        akv = [_mm1(x, y) for x, y in zip(a_ak, vh)]
        atp = [_mm1(t, x) for t, x in zip(tinv, at)]
        uloc = [_mm1(t, x) for t, x in zip(tinv, akv)]
        rp = [x + _mm1(y, z) for x, y, z in zip(rt, a_rb, atp)]
        yloc = [_mm1(x, u) + _mm1(z, w) for x, u, z, w in zip(a_rb, uloc, a_rk, vh)]
        mc = [jnp.where(diag, w, 0.0) + _mm1(jnp.transpose(x), b)
              for w, x, b in zip(w_end, atp, b_end)]
        nc = [_mm1(jnp.transpose(jnp.concatenate([u, w], axis=0)), jnp.concatenate([b, k], axis=0))
              for u, w, b, k in zip(uloc, vh, b_end, k_end)]
        for j in range(cpi):
            for hd in heads:
                src = j * RW_HEADS + hd
                idx = (ip * cpi + j) * RW_HEADS + hd
                m_scr[idx] = mc[src]
                n_scr[idx] = nc[src]
                rp_scr[idx] = rp[src]
            y_ref[rows[j], :] = jnp.concatenate(yloc[j * RW_HEADS:(j + 1) * RW_HEADS], axis=1)
        return carry

    lax.fori_loop(0, tb // (c * cpi), local_body, 0)

    def state_body(ic, carry):
        rows = pl.ds(pl.multiple_of(ic * c, c), c)
        s0 = [s_ref[hd] for hd in heads]
        ys = [_mm1(rp_scr[ic * RW_HEADS + hd], s0[hd], NT) for hd in heads]
        sn = [_mm1(s0[hd], m_scr[ic * RW_HEADS + hd]) + n_scr[ic * RW_HEADS + hd] for hd in heads]
        for hd in heads:
            s_ref[hd] = sn[hd]
        y_ref[rows, :] = y_ref[rows, :] + jnp.concatenate(ys, axis=1)
        return carry

    lax.fori_loop(0, tb // c, state_body, 0)

    y = y_ref[...]
    bd = bd_ref[...]
    inv_n = 1.0 / RW_HEAD
    mean = _head_sum(y, bd) * inv_n
    d = y - mean
    var = _head_sum(d * d, bd) * inv_n
    yn = d * lax.rsqrt(var + RW_GN_EPS) * lnw_ref[...] + lnb_ref[...]
    rk = r_ref[...].astype(F32) * k_ref[...].astype(F32) * rk_ref[...]
    bonus = _head_sum(rk, bd) * v_ref[...].astype(F32)
    o_ref[...] = ((yn + bonus) * g_ref[...].astype(F32)).astype(o_ref.dtype)


def _rwkv(r, lw, k2, v, kk, a, g, lnw, lnb, rk, bd, batch, lp):
    n = r.shape[0]
    tb = _pick_tile(lp, (640, 128))
    nblk = lp // tb
    nmat = (tb // CHUNK) * RW_HEADS
    row = pl.BlockSpec((tb, 512), lambda b, j: (b * nblk + j, 0))
    vec = _const_spec((1, 512))
    mat = pltpu.VMEM((nmat, RW_HEAD, RW_HEAD), F32)
    return pl.pallas_call(
        _rwkv_kernel,
        grid=(batch, nblk),
        in_specs=[row] * 7 + [vec, vec, vec, _const_spec(bd.shape)],
        out_specs=row,
        out_shape=jax.ShapeDtypeStruct((n, 512), BF16),
        scratch_shapes=[pltpu.VMEM((RW_HEADS, RW_HEAD, RW_HEAD), F32), pltpu.VMEM((tb, 512), F32),
                        mat, mat, mat],
        compiler_params=_params(2, parallel=False),
        name="rwkv_chunk",
    )(r, lw, k2, v, kk, a, g, lnw, lnb, rk, bd)


def _attn_kernel(lambda_init, nb, q_ref, k_ref, vt_ref, dl_ref, sub_ref, o_ref, qq_ref, m_ref, l_ref, acc_ref,
                 sa_ref, sb_ref):
    t = ATTN_TILE
    g = ATTN_KV_GROUP
    gt = g * t
    lp = k_ref.shape[1]
    chains = [(bb, slice(h * DA_VDIM, (h + 1) * DA_VDIM)) for bb in range(nb) for h in range(DA_HEADS)]
    ids = range(len(chains))
    qi = pl.program_id(1)
    lane = lax.broadcasted_iota(jnp.int32, (t, 2 * DA_DIM), 1)
    for c, (bb, hs) in enumerate(chains):
        qh = q_ref[bb, :, hs]
        zero = jnp.zeros_like(qh)
        qq_ref[c, 0:t, :] = jnp.where(lane < DA_DIM, qh, zero)
        qq_ref[c, t:2 * t, :] = jnp.where(lane >= DA_DIM, qh, zero)
    m_ref[...] = jnp.full_like(m_ref, MASK_VALUE)
    l_ref[...] = jnp.zeros_like(l_ref)
    acc_ref[...] = jnp.zeros_like(acc_ref)

    def scores(c, row0, nrows):
        bb, hs = chains[c]
        return _dot(k_ref[bb, pl.ds(row0, nrows), hs], qq_ref[c], NT)

    def values(c, row0, nrows):
        bb, hs = chains[c]
        return vt_ref[hs, pl.ds(pl.multiple_of(bb * lp + row0, t), nrows)]

    def update(c, s, pv, anchor=None):
        m_prev = m_ref[c]
        if anchor is not None:
            bits = pltpu.bitcast(anchor, jnp.uint32)
            m_prev = m_prev + pltpu.bitcast((bits >> 16) >> 16, F32)
        m_new = jnp.maximum(m_prev, jnp.max(s, axis=0, keepdims=True))
        alpha = jnp.exp2(m_prev - m_new)
        p = jnp.exp2(s - m_new)
        l_ref[c] = alpha * l_ref[c] + jnp.sum(p, axis=0, keepdims=True)
        acc_ref[c] = alpha * acc_ref[c] + pv(p.astype(BF16))
        m_ref[c] = m_new

    n_grp = jnp.maximum(qi - 1, 0) // g

    blk_row = pl.multiple_of(jnp.maximum(qi - (g - 1), 0) * t, t)
    qrow0 = qi * t + lax.broadcasted_iota(jnp.int32, (t, 2 * t), 1) % t
    krow0 = lax.broadcasted_iota(jnp.int32, (t, 2 * t), 0)
    vis0 = ((krow0 // CHUNK) <= (qrow0 // CHUNK)) & (krow0 >= PAD_FRONT)
    qrow1 = qi * t + lax.broadcasted_iota(jnp.int32, (gt, 2 * t), 1) % t
    krow1 = blk_row + lax.broadcasted_iota(jnp.int32, (gt, 2 * t), 0)
    vis1 = ((krow1 // CHUNK) <= (qrow1 // CHUNK)) & (krow1 >= (1 + n_grp * g) * t)
    for c in ids:
        s = jnp.concatenate([jnp.where(vis0, scores(c, 0, t), MASK_VALUE),
                             jnp.where(vis1, scores(c, blk_row, gt), MASK_VALUE)], axis=0)
        update(c, s, lambda p, c=c: _dot(values(c, 0, t), p[0:t]) + _dot(values(c, blk_row, gt), p[t:]))

    grp_row = lambda i: pl.multiple_of(t + i * gt, t)

    def fill(buf, i):
        for c in ids:
            buf[c] = scores(c, grp_row(i), gt)

    def drain(buf, i):
        for c in ids:
            update(c, buf[c], lambda p, c=c: _dot(values(c, grp_row(i), gt), p))

    @pl.when(n_grp > 0)
    def _():
        fill(sa_ref, 0)

        def pair_body(i, carry):
            fill(sb_ref, 2 * i + 1)
            drain(sa_ref, 2 * i)
            nxt = jnp.minimum(2 * i + 2, n_grp - 1)
            anchors = [None] * ATTN_ANCHOR_LAG
            for c in ids:
                s_new = scores(c, grp_row(nxt), gt)
                sa_ref[c] = s_new
                update(c, sb_ref[c], lambda p, c=c: _dot(values(c, grp_row(2 * i + 1), gt), p), anchors[0])
                anchors = anchors[1:] + [s_new[0:1, :]]
            return carry

        lax.fori_loop(0, n_grp // 2, pair_body, 0)

        @pl.when(n_grp % 2 == 1)
        def _():
            drain(sa_ref, n_grp - 1)

    dl = dl_ref[...]
    lam = (jnp.exp(jnp.sum(dl[0:1] * dl[1:2], axis=1, keepdims=True))
           - jnp.exp(jnp.sum(dl[2:3] * dl[3:4], axis=1, keepdims=True)) + lambda_init)
    for c, (bb, hs) in enumerate(chains):
        ot = acc_ref[c] / l_ref[c]
        ot = ot[:, 0:t] - lam * ot[:, t:2 * t]
        ms = jnp.mean(ot * ot, axis=0, keepdims=True)
        ot = ot * lax.rsqrt(ms + DA_SUBLN_EPS) * sub_ref[...] * (1.0 - lambda_init)
        o_ref[bb, :, hs] = jnp.transpose(ot).astype(o_ref.dtype)


def _attention(q, k, vt, dl, subln_b, lambda_init, batch, lp):
    t = ATTN_TILE
    assert lp >= t + ATTN_KV_GROUP * t
    nb = 2 if batch % 2 == 0 else 1
    nc = nb * DA_HEADS
    once = pl.Buffered(1)
    qspec = pl.BlockSpec((nb, t, 512), lambda b, i: (b, i, 0))
    kspec = pl.BlockSpec((nb, lp, 512), lambda b, i: (b, 0, 0), pipeline_mode=once)
    vspec = pl.BlockSpec((512, nb * lp), lambda b, i: (0, b), pipeline_mode=once)
    scr = pltpu.VMEM((nc, ATTN_KV_GROUP * t, 2 * t), F32)
    out = pl.pallas_call(
        functools.partial(_attn_kernel, lambda_init, nb),
        grid=(batch // nb, lp // t),
        in_specs=[qspec, kspec, vspec, _const_spec(dl.shape), _const_spec(subln_b.shape)],
        out_specs=qspec,
        out_shape=jax.ShapeDtypeStruct((batch, lp, 512), BF16),
        scratch_shapes=[pltpu.VMEM((nc, 2 * t, 2 * DA_DIM), BF16),
                        pltpu.VMEM((nc, 1, 2 * t), F32), pltpu.VMEM((nc, 1, 2 * t), F32),
                        pltpu.VMEM((nc, DA_VDIM, 2 * t), F32), scr, scr],
        compiler_params=_params(2),
        name="diff_attn",
    )(q.reshape(batch, lp, 512), k.reshape(batch, lp, 512), vt, dl, subln_b)
    return out.reshape(batch * lp, 512)


def _merge_kernel(tpb, x_ref, g_ref, u_ref, uh_ref, da_ref, rw_ref, wg_ref, pw_ref, ps_ref, wb_ref, wo_ref, o_ref):
    tm = x_ref.shape[0]
    row = _row_in_batch(tm, tpb, (tm, 1))
    valid = row >= PAD_FRONT
    pos1 = (row - PAD_FRONT + 1).astype(F32)
    u = u_ref[...]
    ext = jnp.concatenate([uh_ref[...], u], axis=0)
    mixed = []
    for gi, w in enumerate(POOL_WINDOWS):
        sl = slice(gi * POOL_GROUP, (gi + 1) * POOL_GROUP)
        s = ext[:, sl]
        shift = 1
        while shift < w:
            s = s + pltpu.roll(s, shift, 0)
            shift *= 2
        cnt = jnp.clip(pos1, 1.0, float(w))
        pooled = s[POOL_HALO:] / cnt - u[:, sl]
        mixed.append(_dot(pooled.astype(BF16), pw_ref[gi]))
    b_pool = jnp.concatenate(mixed, axis=1) * ps_ref[...]
    x = x_ref[...]
    gl = _dot(_rms(x, g_ref[...], NORM_EPS).astype(BF16), wg_ref[...])
    d = D_MODEL
    merged = (jax.nn.sigmoid(gl[:, 0:d]) * _dot(b_pool.astype(BF16), wb_ref[0])
              + jax.nn.sigmoid(gl[:, d:2 * d]) * _dot(da_ref[...], wb_ref[1])
              + jax.nn.sigmoid(gl[:, 2 * d:3 * d]) * _dot(rw_ref[...], wb_ref[2]))
    xn = x + _dot(merged.astype(BF16), wo_ref[...])
    o_ref[...] = jnp.where(valid, xn, 0.0)


def _merge(x, g, u, da, rw, wg, pw, ps, wb, wo, lp):
    n = x.shape[0]
    tm = _pick_tile(lp, (320, 128))
    tpb = lp // tm
    row = lambda width: pl.BlockSpec((tm, width), lambda i: (i, 0))
    return pl.pallas_call(
        functools.partial(_merge_kernel, tpb),
        grid=(n // tm,),
        in_specs=[row(D_MODEL), _const_spec((1, D_MODEL)), row(512), _halo_spec(tm, POOL_HALO, 512), row(512),
                  row(512), _weight_spec(wg.shape), _const_spec(pw.shape), _const_spec((1, 512)),
                  _weight_spec(wb.shape), _weight_spec(wo.shape)],
        out_specs=row(D_MODEL),
        out_shape=jax.ShapeDtypeStruct((n, D_MODEL), F32),
        compiler_params=_params(),
        name="merge",
    )(x, g, u, u, da, rw, wg, pw, ps, wb, wo)


def _ffn_kernel(tpb, x_ref, xh_ref, g_ref, wu_ref, cw_ref, wd_ref, *rest):
    gf_ref, o_ref = rest if len(rest) == 2 else (None, rest[0])
    tm = x_ref.shape[0]
    x = x_ref[...]
    x_ext = jnp.concatenate([xh_ref[...], x], axis=0)
    h = _rms(x_ext, g_ref[...], NORM_EPS).astype(BF16)

    def conv(col, width):
        u = _dot(h, wu_ref[:, col:col + width])
        cw = cw_ref[:, col:col + width]
        out = (u * cw[2:3] + pltpu.roll(u, 1, 0) * cw[1:2] + pltpu.roll(u, 2, 0) * cw[0:1])
        return out[ROW_HALO:]

    acc = x
    for c0, width in FF_CHUNKS:
        gate = conv(c0, width)
        up = conv(D_FF + c0, width)
        act = (gate * jax.nn.sigmoid(gate) * up).astype(BF16)
        acc = acc + _dot(act, wd_ref[c0:c0 + width, :])
    valid = _row_in_batch(tm, tpb, (tm, 1)) >= PAD_FRONT
    res = jnp.where(valid, acc, 0.0)
    o_ref[...] = res if gf_ref is None else _rms(res, gf_ref[...], NORM_EPS)


def _ffn(x, g, wu, cw, wd, lp, g_final=None):
    n = x.shape[0]
    tm = _pick_tile(lp, (640, 320, 128))
    tpb = lp // tm
    row = pl.BlockSpec((tm, D_MODEL), lambda i: (i, 0))
    ins = [x, x, g, wu, cw, wd]
    specs = [row, _halo_spec(tm, ROW_HALO, D_MODEL), _const_spec((1, D_MODEL)),
             _weight_spec(wu.shape), _const_spec(cw.shape), _weight_spec(wd.shape)]
    if g_final is not None:
        ins.append(g_final)
        specs.append(_const_spec((1, D_MODEL)))
    return pl.pallas_call(
        functools.partial(_ffn_kernel, tpb),
        grid=(n // tm,),
        in_specs=specs,
        out_specs=row,
        out_shape=jax.ShapeDtypeStruct((n, D_MODEL), F32),
        compiler_params=_params(),
        name="conv_glu_ffn",
    )(*ins)


def _rope_tables(lp):
    half = DA_ROPE // 2
    pos = (jnp.arange(lp, dtype=jnp.int32) - PAD_FRONT).astype(F32)
    inv = ROPE_THETA ** (-jnp.arange(0, DA_ROPE, 2, dtype=F32) / DA_ROPE)
    ang = pos[:, None] * inv[None, :]
    cos, sin = jnp.cos(ang), jnp.sin(ang)
    ones = jnp.ones((lp, DA_DIM - DA_ROPE), F32)
    zeros = jnp.zeros((lp, DA_DIM - DA_ROPE), F32)
    zh = jnp.zeros((lp, half), F32)
    rc = jnp.concatenate([cos, cos, ones], axis=1)
    rs1 = jnp.concatenate([-sin, zh, zeros], axis=1)
    rs2 = jnp.concatenate([zh, sin, zeros], axis=1)
    return tuple(jnp.concatenate([t, t], axis=1) for t in (rc, rs1, rs2))


def kernel(x, meta_tokens, norm_mix, norm_ffn, norm_final, w_in, pool_w, pool_scale, da_lambda, da_subln,
           rw_mu, rw_w0, rw_w2, rw_a0, rw_a2, rw_g2, rw_k_k, rw_k_a, rw_r_k, rw_lnx_w, rw_lnx_b,
           w_branch, w_out, ffn_up, ffn_conv, ffn_down):
    batch, seq, d = x.shape
    depth = w_in.shape[0]
    assert d == D_MODEL and seq % ATTN_TILE == 0 and meta_tokens.shape == (N_META, D_MODEL)
    lp = seq + ATTN_TILE
    n = batch * lp

    meta = jnp.broadcast_to(meta_tokens[None].astype(x.dtype), (batch, N_META, d))
    xs = jnp.concatenate([jnp.zeros((batch, PAD_FRONT, d), x.dtype), meta, x], axis=1).reshape(n, d)

    rc, rs1, rs2 = _rope_tables(lp)
    hi = lax.broadcasted_iota(jnp.int32, (LANES, LANES), 0) // RW_HEAD
    hj = lax.broadcasted_iota(jnp.int32, (LANES, LANES), 1) // RW_HEAD
    bd = (hi == hj).astype(BF16)
    lora_zeros = jnp.zeros((RW_LORA_SLAB // 2, RW_WIDTH), BF16)
    row = lambda v: v.reshape(1, -1).astype(F32)

    for l in range(depth):
        lambda_init = 0.8 - 0.6 * math.exp(-0.3 * l)
        w_l = w_in[l].astype(BF16)
        g_mix = row(norm_mix[l])
        u_pool, q, k, vt = _proj_attn(xs, g_mix, w_l[:, 0:1536], w_l[:, 1536:2048].T, rc, rs1, rs2, lp)
        w2p = jnp.concatenate([rw_w2[l].astype(BF16), lora_zeros], axis=0)
        a2p = jnp.concatenate([lora_zeros, rw_a2[l].astype(BF16)], axis=0)
        prep = _rw_prep(xs, g_mix, w_l[:, 2048:2048 + RW_SHIFT_WIDTH], row(rw_mu[l]), row(rw_w0[l]), w2p,
                        row(rw_a0[l]), a2p, rw_g2[l].astype(BF16), row(rw_k_k[l]), row(rw_k_a[l]), bd, lp)
        b_rw = _rwkv(*prep, row(rw_lnx_w[l]), row(rw_lnx_b[l]), row(rw_r_k[l]), bd, batch, lp)
        subln_b = jnp.broadcast_to(da_subln[l].astype(F32)[:, None], (DA_VDIM, ATTN_TILE))
        b_da = _attention(q, k, vt, da_lambda[l].astype(F32), subln_b, lambda_init, batch, lp)
        xs = _merge(xs, g_mix, u_pool, b_da, b_rw, w_l[:, 2048 + RW_SHIFT_WIDTH:], pool_w[l].astype(BF16), row(pool_scale[l]),
                    w_branch[l].astype(BF16), w_out[l].astype(BF16), lp)
        xs = _ffn(xs, row(norm_ffn[l]), ffn_up[l].astype(BF16), ffn_conv[l].astype(F32),
                  ffn_down[l].astype(BF16), lp, g_final=row(norm_final) if l == depth - 1 else None)

    return xs.reshape(batch, lp, d)[:, ATTN_TILE:]
```

```python
import functools
import math

import jax
import jax.numpy as jnp
from jax import lax
from jax.experimental import pallas as pl
from jax.experimental.pallas import tpu as pltpu

F32 = jnp.float32
BF16 = jnp.bfloat16

D_MODEL = 1024
N_META = 16
CHUNK = 64
ROPE_THETA = 500000.0
NORM_EPS = 1e-6
POOL_WINDOWS = (2, 4, 8, 16)
POOL_GROUP = 128
POOL_WIDTH = 512
DA_HEADS = 4
DA_DIM = 64
DA_VDIM = 128
DA_ROPE = 16
DA_SUBLN_EPS = 1e-5
RW_HEAD = 64
RW_WIDTH = 512
RW_HEADS = 8
RW_LORA_SLAB = 128
RW_GATE_LORA = 128
RW_SHIFT_WIDTH = 1792
RW_GN_EPS = 64e-5
RW_CHUNKS_PER_ITER = (5, 2, 1)
N_BRANCH = 3
D_FF = 2816
FF_CHUNKS = ((0, 1536), (1536, 1280))

LANES = 128
SUBLANES = 8
VMEM_LIMIT_BYTES = 56 * 1024 * 1024

ATTN_TILE = 128
ATTN_KV_GROUP = 4
ATTN_ANCHOR_LAG = 2
PAD_FRONT = ATTN_TILE - N_META
LOG2_E = math.log2(math.e)
MASK_VALUE = -1e30
POOL_HALO = 16
ROW_HALO = SUBLANES

NN = (((1,), (0,)), ((), ()))
NT = (((1,), (1,)), ((), ()))


def _dot(a, b, dn=NN):
    return lax.dot_general(a, b, dn, preferred_element_type=F32)


def _split(x):
    hi = x.astype(BF16)
    lo = (x - hi.astype(F32)).astype(BF16)
    return hi, lo


def _mm1(a, b, dn=NN):
    return _dot(a.astype(BF16), b.astype(BF16), dn)


def _mm2_exact_rhs(a, b_bf16):
    ah, al = _split(a)
    return _dot(ah, b_bf16) + _dot(al, b_bf16)


def _mm2_exact_lhs(a_bf16, b):
    bh, bl = _split(b)
    return _dot(a_bf16, bh) + _dot(a_bf16, bl)


def _rms(x, g, eps):
    ms = jnp.mean(x * x, axis=-1, keepdims=True)
    return x * lax.rsqrt(ms + eps) * g


def _row_in_batch(tm, tiles_per_batch, shape):
    t = pl.program_id(0) % tiles_per_batch
    return t * tm + lax.broadcasted_iota(jnp.int32, shape, 0)


def _params(n_axes=1, parallel=True):
    sem = ("parallel" if parallel else "arbitrary",) * n_axes
    return pltpu.CompilerParams(dimension_semantics=sem, vmem_limit_bytes=VMEM_LIMIT_BYTES)


def _const_spec(shape):
    nd = len(shape)
    return pl.BlockSpec(shape, lambda *_: (0,) * nd)


def _weight_spec(shape):
    nd = len(shape)
    return pl.BlockSpec(shape, lambda *_: (0,) * nd, pipeline_mode=pl.Buffered(1))


def _halo_spec(tm, halo, width):
    per = tm // halo
    return pl.BlockSpec((halo, width), lambda i: (jnp.maximum(i * per - 1, 0), 0))


def _pick_tile(lp, candidates):
    for c in candidates:
        if lp % c == 0:
            return c
    raise ValueError(f"no row tile for padded length {lp}")


def _proj_attn_kernel(x_ref, g_ref, w_ref, wvt_ref, rc_ref, rs1_ref, rs2_ref, u_ref, q_ref, k_ref, vt_ref):
    h = _rms(x_ref[...], g_ref[...], NORM_EPS).astype(BF16)
    p = _dot(h, w_ref[...])
    rep = POOL_WIDTH // LANES
    rc = jnp.concatenate([rc_ref[...]] * rep, axis=1)
    rs1 = jnp.concatenate([rs1_ref[...]] * rep, axis=1)
    rs2 = jnp.concatenate([rs2_ref[...]] * rep, axis=1)

    def rope(t):
        up = pltpu.roll(t, POOL_WIDTH - DA_ROPE // 2, 1)
        dn = pltpu.roll(t, DA_ROPE // 2, 1)
        return t * rc + up * rs1 + dn * rs2

    u_ref[...] = p[:, 0:512]
    q_ref[...] = (rope(p[:, 512:1024]) * (DA_DIM ** -0.5 * LOG2_E)).astype(BF16)
    k_ref[...] = rope(p[:, 1024:1536]).astype(BF16)
    vt_ref[...] = _dot(wvt_ref[...], h, NT).astype(BF16)


def _proj_attn(x, g, w, wvt, rc, rs1, rs2, lp):
    n = x.shape[0]
    tm = _pick_tile(lp, (640, 128))
    tpb = lp // tm
    row = lambda width: pl.BlockSpec((tm, width), lambda i: (i, 0))
    tab = pl.BlockSpec((tm, LANES), lambda i: (i % tpb, 0))
    return pl.pallas_call(
        _proj_attn_kernel,
        grid=(n // tm,),
        in_specs=[row(D_MODEL), _const_spec((1, D_MODEL)), _weight_spec(w.shape), _weight_spec(wvt.shape),
                  tab, tab, tab],
        out_specs=[row(512)] * 3 + [pl.BlockSpec((512, tm), lambda i: (0, i))],
        out_shape=[jax.ShapeDtypeStruct((n, 512), F32)] + [jax.ShapeDtypeStruct((n, 512), BF16)] * 2
        + [jax.ShapeDtypeStruct((512, n), BF16)],
        compiler_params=_params(),
        name="proj_attn",
    )(x, g, w, wvt, rc, rs1, rs2)


def _head_sum(x, bd, passes=2):
    one = (lambda v: _dot(v.astype(BF16), bd)) if passes == 1 else (lambda v: _mm2_exact_rhs(v, bd))
    return jnp.concatenate([one(x[:, s:s + LANES]) for s in range(0, x.shape[1], LANES)], axis=1)


def _rw_prep_kernel(tpb, x_ref, xh_ref, g_ref, w_ref, mu_ref, w0_ref, w2_ref, a0_ref, a2_ref, g2_ref,
                    kk_ref, ka_ref, bd_ref, r_o, lw_o, k_o, v_o, kk_o, a_o, g_o):
    tm = x_ref.shape[0]
    x_ext = jnp.concatenate([xh_ref[...], x_ref[...]], axis=0)
    h = _rms(x_ext, g_ref[...], NORM_EPS).astype(BF16)
    p_ext = _dot(h, w_ref[...])
    p = p_ext[ROW_HALO:]
    p_prev = pltpu.roll(p_ext, 1, 0)[ROW_HALO:]
    valid = _row_in_batch(tm, tpb, (tm, 1)) >= PAD_FRONT
    pm = jnp.where(valid, p + mu_ref[...] * (p_prev - p), 0.0)

    r = pm[:, 0:512]
    k = pm[:, 512:1024]
    v = pm[:, 1024:1536]
    lora = pm[:, 1536:1536 + RW_LORA_SLAB]
    gl = pm[:, 1536 + RW_LORA_SLAB:RW_SHIFT_WIDTH]

    z = w0_ref[...] + _dot(jnp.tanh(lora).astype(BF16), w2_ref[...])
    softplus_neg = jnp.maximum(-z, 0.0) + jnp.log1p(jnp.exp(-jnp.abs(z)))
    w_log = -softplus_neg - 0.5
    lw = jnp.where(valid, -jnp.exp(w_log), 0.0)
    a = jax.nn.sigmoid(a0_ref[...] + _dot(lora.astype(BF16), a2_ref[...]))
    g = _dot(jax.nn.sigmoid(gl).astype(BF16), g2_ref[...])
    kk = k * kk_ref[...]
    nrm = jnp.sqrt(_head_sum(kk * kk, bd_ref[...], passes=1))
    kk = kk / jnp.maximum(nrm, 1e-12)
    k2 = k * (1.0 + (a - 1.0) * ka_ref[...])

    r_o[...] = r.astype(BF16)
    lw_o[...] = lw
    k_o[...] = k2.astype(BF16)
    v_o[...] = v.astype(BF16)
    kk_o[...] = kk.astype(BF16)
    a_o[...] = a.astype(BF16)
    g_o[...] = g.astype(BF16)


def _rw_prep(x, g, w, mu, w0, w2p, a0, a2p, g2, k_k, k_a, bd, lp):
    n = x.shape[0]
    tm = _pick_tile(lp, (640, 320, 128))
    tpb = lp // tm
    vec = lambda width: _const_spec((1, width))
    row512 = pl.BlockSpec((tm, 512), lambda i: (i, 0))
    return pl.pallas_call(
        functools.partial(_rw_prep_kernel, tpb),
        grid=(n // tm,),
        in_specs=[pl.BlockSpec((tm, D_MODEL), lambda i: (i, 0)), _halo_spec(tm, ROW_HALO, D_MODEL),
                  vec(D_MODEL), _weight_spec(w.shape), vec(RW_SHIFT_WIDTH), vec(512),
                  _const_spec(w2p.shape), vec(512), _const_spec(a2p.shape), _const_spec(g2.shape),
                  vec(512), vec(512), _const_spec(bd.shape)],
        out_specs=[row512] * 7,
        out_shape=[jax.ShapeDtypeStruct((n, 512), F32 if i == 1 else BF16) for i in range(7)],
        compiler_params=_params(),
        name="rw_prep",
    )(x, x, g, w, mu, w0, w2p, a0, a2p, g2, k_k, k_a, bd)


def _unit_lower_inverse(nmats, masks):
    eye, blk16, m32, m64 = masks
    ps = [jnp.where(blk16, n, 0.0) for n in nmats]
    ts = [eye + p for p in ps]
    for _ in range(3):
        ps = [_mm1(p, p) for p in ps]
        ts = [t + _mm1(t, p) for t, p in zip(ts, ps)]
    for m in (m32, m64):
        xs = [_mm1(t, jnp.where(m, n, 0.0)) for t, n in zip(ts, nmats)]
        ts = [t + _mm1(x, t) for t, x in zip(ts, xs)]
    return ts


def _rwkv_kernel(r_ref, lw_ref, k_ref, v_ref, kk_ref, a_ref, g_ref, lnw_ref, lnb_ref, rk_ref, bd_ref,
                 o_ref, s_ref, y_ref, m_scr, n_scr, rp_scr):
    tb = r_ref.shape[0]
    c = CHUNK
    heads = range(RW_HEADS)
    hsl = [slice(hd * RW_HEAD, (hd + 1) * RW_HEAD) for hd in heads]

    @pl.when(pl.program_id(1) == 0)
    def _():
        s_ref[...] = jnp.zeros_like(s_ref)

    ri = lax.broadcasted_iota(jnp.int32, (c, c), 0)
    ci = lax.broadcasted_iota(jnp.int32, (c, c), 1)
    strict = ri > ci
    incl = ri >= ci
    diag = ri == ci
    eye = jnp.where(diag, 1.0, 0.0).astype(F32)
    blk16 = (ri // 16) == (ci // 16)
    m32 = ((ri // 32) == (ci // 32)) & ((ri // 16) > (ci // 16))
    m64 = (ri >= 32) & (ci < 32)
    masks = (eye, blk16, m32, m64)
    ltri = jnp.where(incl, 1.0, 0.0).astype(BF16)

    cpi = next(d for d in RW_CHUNKS_PER_ITER if (tb // c) % d == 0)

    def local_body(ip, carry):
        at, rt, bt, kt, b_end, k_end, w_end, vh, rows = [], [], [], [], [], [], [], [], []
        for j in range(cpi):
            rw = pl.ds(pl.multiple_of((ip * cpi + j) * c, c), c)
            lw = lw_ref[rw, :]
            kk = kk_ref[rw, :].astype(F32)
            cum = _mm2_exact_lhs(ltri, lw)
            w_t = jnp.exp(cum)
            w_inv = jnp.exp(-cum)
            we = w_t[c - 1:c, :]
            a_t = -kk * jnp.exp(cum - lw)
            r_t = r_ref[rw, :].astype(F32) * w_t
            b_t = kk * a_ref[rw, :].astype(F32) * w_inv
            k_t = k_ref[rw, :].astype(F32) * w_inv
            v = v_ref[rw, :].astype(F32)
            for sl in hsl:
                at.append(a_t[:, sl])
                rt.append(r_t[:, sl])
                bt.append(b_t[:, sl])
                kt.append(k_t[:, sl])
                b_end.append(b_t[:, sl] * we[:, sl])
                k_end.append(k_t[:, sl] * we[:, sl])
                w_end.append(we[:, sl])
                vh.append(v[:, sl])
            rows.append(rw)

        lhs = [jnp.concatenate([x, y], axis=0) for x, y in zip(at, rt)]
        gb = [_mm1(l, x, NT) for l, x in zip(lhs, bt)]
        gk = [_mm1(l, x, NT) for l, x in zip(lhs, kt)]
        a_ab = [jnp.where(strict, g[:c], 0.0) for g in gb]
        a_ak = [jnp.where(strict, g[:c], 0.0) for g in gk]
        a_rb = [jnp.where(incl, g[c:], 0.0) for g in gb]
        a_rk = [jnp.where(incl, g[c:], 0.0) for g in gk]
        tinv = _unit_lower_inverse(a_ab, masks)
        akv = [_mm1(x, y) for x, y in zip(a_ak, vh)]
        atp = [_mm1(t, x) for t, x in zip(tinv, at)]
        uloc = [_mm1(t, x) for t, x in zip(tinv, akv)]
        rp = [x + _mm1(y, z) for x, y, z in zip(rt, a_rb, atp)]
        yloc = [_mm1(x, u) + _mm1(z, w) for x, u, z, w in zip(a_rb, uloc, a_rk, vh)]
        mc = [jnp.where(diag, w, 0.0) + _mm1(jnp.transpose(x), b)
              for w, x, b in zip(w_end, atp, b_end)]
        nc = [_mm1(jnp.transpose(jnp.concatenate([u, w], axis=0)), jnp.concatenate([b, k], axis=0))
              for u, w, b, k in zip(uloc, vh, b_end, k_end)]
        for j in range(cpi):
            for hd in heads:
                src = j * RW_HEADS + hd
                idx = (ip * cpi + j) * RW_HEADS + hd
                m_scr[idx] = mc[src]
                n_scr[idx] = nc[src]
                rp_scr[idx] = rp[src]
            y_ref[rows[j], :] = jnp.concatenate(yloc[j * RW_HEADS:(j + 1) * RW_HEADS], axis=1)
        return carry

    lax.fori_loop(0, tb // (c * cpi), local_body, 0)

    def state_body(ic, carry):
        rows = pl.ds(pl.multiple_of(ic * c, c), c)
        s0 = [s_ref[hd] for hd in heads]
        ys = [_mm1(rp_scr[ic * RW_HEADS + hd], s0[hd], NT) for hd in heads]
        sn = [_mm1(s0[hd], m_scr[ic * RW_HEADS + hd]) + n_scr[ic * RW_HEADS + hd] for hd in heads]
        for hd in heads:
            s_ref[hd] = sn[hd]
        y_ref[rows, :] = y_ref[rows, :] + jnp.concatenate(ys, axis=1)
        return carry

    lax.fori_loop(0, tb // c, state_body, 0)

    y = y_ref[...]
    bd = bd_ref[...]
    inv_n = 1.0 / RW_HEAD
    mean = _head_sum(y, bd) * inv_n
    d = y - mean
    var = _head_sum(d * d, bd) * inv_n
    yn = d * lax.rsqrt(var + RW_GN_EPS) * lnw_ref[...] + lnb_ref[...]
    rk = r_ref[...].astype(F32) * k_ref[...].astype(F32) * rk_ref[...]
    bonus = _head_sum(rk, bd) * v_ref[...].astype(F32)
    o_ref[...] = ((yn + bonus) * g_ref[...].astype(F32)).astype(o_ref.dtype)


def _rwkv(r, lw, k2, v, kk, a, g, lnw, lnb, rk, bd, batch, lp):
    n = r.shape[0]
    tb = _pick_tile(lp, (640, 128))
    nblk = lp // tb
    nmat = (tb // CHUNK) * RW_HEADS
    row = pl.BlockSpec((tb, 512), lambda b, j: (b * nblk + j, 0))
    vec = _const_spec((1, 512))
    mat = pltpu.VMEM((nmat, RW_HEAD, RW_HEAD), F32)
    return pl.pallas_call(
        _rwkv_kernel,
        grid=(batch, nblk),
        in_specs=[row] * 7 + [vec, vec, vec, _const_spec(bd.shape)],
        out_specs=row,
        out_shape=jax.ShapeDtypeStruct((n, 512), BF16),
        scratch_shapes=[pltpu.VMEM((RW_HEADS, RW_HEAD, RW_HEAD), F32), pltpu.VMEM((tb, 512), F32),
                        mat, mat, mat],
        compiler_params=_params(2, parallel=False),
        name="rwkv_chunk",
    )(r, lw, k2, v, kk, a, g, lnw, lnb, rk, bd)


def _attn_kernel(lambda_init, nb, q_ref, k_ref, vt_ref, dl_ref, sub_ref, o_ref, qq_ref, m_ref, l_ref, acc_ref,
                 sa_ref, sb_ref):
    t = ATTN_TILE
    g = ATTN_KV_GROUP
    gt = g * t
    lp = k_ref.shape[1]
    chains = [(bb, slice(h * DA_VDIM, (h + 1) * DA_VDIM)) for bb in range(nb) for h in range(DA_HEADS)]
    ids = range(len(chains))
    qi = pl.program_id(1)
    lane = lax.broadcasted_iota(jnp.int32, (t, 2 * DA_DIM), 1)
    for c, (bb, hs) in enumerate(chains):
        qh = q_ref[bb, :, hs]
        zero = jnp.zeros_like(qh)
        qq_ref[c, 0:t, :] = jnp.where(lane < DA_DIM, qh, zero)
        qq_ref[c, t:2 * t, :] = jnp.where(lane >= DA_DIM, qh, zero)
    m_ref[...] = jnp.full_like(m_ref, MASK_VALUE)
    l_ref[...] = jnp.zeros_like(l_ref)
    acc_ref[...] = jnp.zeros_like(acc_ref)

    def scores(c, row0, nrows):
        bb, hs = chains[c]
        return _dot(k_ref[bb, pl.ds(row0, nrows), hs], qq_ref[c], NT)

    def values(c, row0, nrows):
        bb, hs = chains[c]
        return vt_ref[hs, pl.ds(pl.multiple_of(bb * lp + row0, t), nrows)]

    def update(c, s, pv, anchor=None):
        m_prev = m_ref[c]
        if anchor is not None:
            bits = pltpu.bitcast(anchor, jnp.uint32)
            m_prev = m_prev + pltpu.bitcast((bits >> 16) >> 16, F32)
        m_new = jnp.maximum(m_prev, jnp.max(s, axis=0, keepdims=True))
        alpha = jnp.exp2(m_prev - m_new)
        p = jnp.exp2(s - m_new)
        l_ref[c] = alpha * l_ref[c] + jnp.sum(p, axis=0, keepdims=True)
        acc_ref[c] = alpha * acc_ref[c] + pv(p.astype(BF16))
        m_ref[c] = m_new

    n_grp = jnp.maximum(qi - 1, 0) // g
    grp_row = lambda i: pl.multiple_of(t + i * gt, t)

    blk_row = pl.multiple_of(jnp.maximum(qi - (g - 1), 0) * t, t)
    qrow0 = qi * t + lax.broadcasted_iota(jnp.int32, (t, 2 * t), 1) % t
    krow0 = lax.broadcasted_iota(jnp.int32, (t, 2 * t), 0)
    vis0 = ((krow0 // CHUNK) <= (qrow0 // CHUNK)) & (krow0 >= PAD_FRONT)
    qrow1 = qi * t + lax.broadcasted_iota(jnp.int32, (gt, 2 * t), 1) % t
    krow1 = blk_row + lax.broadcasted_iota(jnp.int32, (gt, 2 * t), 0)
    vis1 = ((krow1 // CHUNK) <= (qrow1 // CHUNK)) & (krow1 >= (1 + n_grp * g) * t)

    def drain_edge(block_scores):
        for c in ids:
            s = jnp.concatenate([jnp.where(vis0, scores(c, 0, t), MASK_VALUE),
                                 jnp.where(vis1, block_scores(c), MASK_VALUE)], axis=0)
            update(c, s, lambda p, c=c: _dot(values(c, 0, t), p[0:t]) + _dot(values(c, blk_row, gt), p[t:]))

    def fill(buf, row0):
        for c in ids:
            buf[c] = scores(c, row0, gt)

    def drain(buf, i):
        for c in ids:
            update(c, buf[c], lambda p, c=c: _dot(values(c, grp_row(i), gt), p))

    @pl.when(n_grp == 0)
    def _():
        drain_edge(lambda c: scores(c, blk_row, gt))

    @pl.when(n_grp > 0)
    def _():
        fill(sa_ref, grp_row(0))

        def pair_body(i, carry):
            fill(sb_ref, grp_row(2 * i + 1))
            drain(sa_ref, 2 * i)
            nxt_row = jnp.where(2 * i + 2 < n_grp, grp_row(2 * i + 2), blk_row)
            anchors = [None] * ATTN_ANCHOR_LAG
            for c in ids:
                s_new = scores(c, pl.multiple_of(nxt_row, t), gt)
                sa_ref[c] = s_new
                update(c, sb_ref[c], lambda p, c=c: _dot(values(c, grp_row(2 * i + 1), gt), p), anchors[0])
                anchors = anchors[1:] + [s_new[0:1, :]]
            return carry

        lax.fori_loop(0, n_grp // 2, pair_body, 0)

        @pl.when(n_grp % 2 == 1)
        def _():
            fill(sb_ref, blk_row)
            drain(sa_ref, n_grp - 1)
            drain_edge(lambda c: sb_ref[c])

        @pl.when(n_grp % 2 == 0)
        def _():
            drain_edge(lambda c: sa_ref[c])

    dl = dl_ref[...]
    lam = (jnp.exp(jnp.sum(dl[0:1] * dl[1:2], axis=1, keepdims=True))
           - jnp.exp(jnp.sum(dl[2:3] * dl[3:4], axis=1, keepdims=True)) + lambda_init)
    for c, (bb, hs) in enumerate(chains):
        ot = acc_ref[c] / l_ref[c]
        ot = ot[:, 0:t] - lam * ot[:, t:2 * t]
        ms = jnp.mean(ot * ot, axis=0, keepdims=True)
        ot = ot * lax.rsqrt(ms + DA_SUBLN_EPS) * sub_ref[...] * (1.0 - lambda_init)
        o_ref[bb, :, hs] = jnp.transpose(ot).astype(o_ref.dtype)


def _attention(q, k, vt, dl, subln_b, lambda_init, batch, lp):
    t = ATTN_TILE
    assert lp >= t + ATTN_KV_GROUP * t
    nb = 2 if batch % 2 == 0 else 1
    nc = nb * DA_HEADS
    once = pl.Buffered(1)
    qspec = pl.BlockSpec((nb, t, 512), lambda b, i: (b, i, 0))
    kspec = pl.BlockSpec((nb, lp, 512), lambda b, i: (b, 0, 0), pipeline_mode=once)
    vspec = pl.BlockSpec((512, nb * lp), lambda b, i: (0, b), pipeline_mode=once)
    scr = pltpu.VMEM((nc, ATTN_KV_GROUP * t, 2 * t), F32)
    out = pl.pallas_call(
        functools.partial(_attn_kernel, lambda_init, nb),
        grid=(batch // nb, lp // t),
        in_specs=[qspec, kspec, vspec, _const_spec(dl.shape), _const_spec(subln_b.shape)],
        out_specs=qspec,
        out_shape=jax.ShapeDtypeStruct((batch, lp, 512), BF16),
        scratch_shapes=[pltpu.VMEM((nc, 2 * t, 2 * DA_DIM), BF16),
                        pltpu.VMEM((nc, 1, 2 * t), F32), pltpu.VMEM((nc, 1, 2 * t), F32),
                        pltpu.VMEM((nc, DA_VDIM, 2 * t), F32), scr, scr],
        compiler_params=_params(2),
        name="diff_attn",
    )(q.reshape(batch, lp, 512), k.reshape(batch, lp, 512), vt, dl, subln_b)
    return out.reshape(batch * lp, 512)


def _merge_kernel(tpb, x_ref, g_ref, u_ref, uh_ref, da_ref, rw_ref, wg_ref, pw_ref, ps_ref, wb_ref, wo_ref, o_ref):
    tm = x_ref.shape[0]
    row = _row_in_batch(tm, tpb, (tm, 1))
    valid = row >= PAD_FRONT
    pos1 = (row - PAD_FRONT + 1).astype(F32)
    u = u_ref[...]
    ext = jnp.concatenate([uh_ref[...], u], axis=0)
    mixed = []
    for gi, w in enumerate(POOL_WINDOWS):
        sl = slice(gi * POOL_GROUP, (gi + 1) * POOL_GROUP)
        s = ext[:, sl]
        shift = 1
        while shift < w:
            s = s + pltpu.roll(s, shift, 0)
            shift *= 2
        cnt = jnp.clip(pos1, 1.0, float(w))
        pooled = s[POOL_HALO:] / cnt - u[:, sl]
        mixed.append(_dot(pooled.astype(BF16), pw_ref[gi]))
    b_pool = jnp.concatenate(mixed, axis=1) * ps_ref[...]
    x = x_ref[...]
    gl = _dot(_rms(x, g_ref[...], NORM_EPS).astype(BF16), wg_ref[...])
    d = D_MODEL
    merged = (jax.nn.sigmoid(gl[:, 0:d]) * _dot(b_pool.astype(BF16), wb_ref[0])
              + jax.nn.sigmoid(gl[:, d:2 * d]) * _dot(da_ref[...], wb_ref[1])
              + jax.nn.sigmoid(gl[:, 2 * d:3 * d]) * _dot(rw_ref[...], wb_ref[2]))
    xn = x + _dot(merged.astype(BF16), wo_ref[...])
    o_ref[...] = jnp.where(valid, xn, 0.0)


def _merge(x, g, u, da, rw, wg, pw, ps, wb, wo, lp):
    n = x.shape[0]
    tm = _pick_tile(lp, (320, 128))
    tpb = lp // tm
    row = lambda width: pl.BlockSpec((tm, width), lambda i: (i, 0))
    return pl.pallas_call(
        functools.partial(_merge_kernel, tpb),
        grid=(n // tm,),
        in_specs=[row(D_MODEL), _const_spec((1, D_MODEL)), row(512), _halo_spec(tm, POOL_HALO, 512), row(512),
                  row(512), _weight_spec(wg.shape), _const_spec(pw.shape), _const_spec((1, 512)),
                  _weight_spec(wb.shape), _weight_spec(wo.shape)],
        out_specs=row(D_MODEL),
        out_shape=jax.ShapeDtypeStruct((n, D_MODEL), F32),
        compiler_params=_params(),
        name="merge",
    )(x, g, u, u, da, rw, wg, pw, ps, wb, wo)


def _ffn_kernel(tpb, x_ref, xh_ref, g_ref, wu_ref, cw_ref, wd_ref, *rest):
    gf_ref, o_ref = rest if len(rest) == 2 else (None, rest[0])
    tm = x_ref.shape[0]
    x = x_ref[...]
    x_ext = jnp.concatenate([xh_ref[...], x], axis=0)
    h = _rms(x_ext, g_ref[...], NORM_EPS).astype(BF16)

    def conv(col, width):
        u = _dot(h, wu_ref[:, col:col + width])
        cw = cw_ref[:, col:col + width]
        out = (u * cw[2:3] + pltpu.roll(u, 1, 0) * cw[1:2] + pltpu.roll(u, 2, 0) * cw[0:1])
        return out[ROW_HALO:]

    acc = x
    for c0, width in FF_CHUNKS:
        gate = conv(c0, width)
        up = conv(D_FF + c0, width)
        act = (gate * jax.nn.sigmoid(gate) * up).astype(BF16)
        acc = acc + _dot(act, wd_ref[c0:c0 + width, :])
    valid = _row_in_batch(tm, tpb, (tm, 1)) >= PAD_FRONT
    res = jnp.where(valid, acc, 0.0)
    o_ref[...] = res if gf_ref is None else _rms(res, gf_ref[...], NORM_EPS)


def _ffn(x, g, wu, cw, wd, lp, g_final=None):
    n = x.shape[0]
    tm = _pick_tile(lp, (640, 320, 128))
    tpb = lp // tm
    row = pl.BlockSpec((tm, D_MODEL), lambda i: (i, 0))
    ins = [x, x, g, wu, cw, wd]
    specs = [row, _halo_spec(tm, ROW_HALO, D_MODEL), _const_spec((1, D_MODEL)),
             _weight_spec(wu.shape), _const_spec(cw.shape), _weight_spec(wd.shape)]
    if g_final is not None:
        ins.append(g_final)
        specs.append(_const_spec((1, D_MODEL)))
    return pl.pallas_call(
        functools.partial(_ffn_kernel, tpb),
        grid=(n // tm,),
        in_specs=specs,
        out_specs=row,
        out_shape=jax.ShapeDtypeStruct((n, D_MODEL), F32),
        compiler_params=_params(),
        name="conv_glu_ffn",
    )(*ins)


def _rope_tables(lp):
    half = DA_ROPE // 2
    pos = (jnp.arange(lp, dtype=jnp.int32) - PAD_FRONT).astype(F32)
    inv = ROPE_THETA ** (-jnp.arange(0, DA_ROPE, 2, dtype=F32) / DA_ROPE)
    ang = pos[:, None] * inv[None, :]
    cos, sin = jnp.cos(ang), jnp.sin(ang)
    ones = jnp.ones((lp, DA_DIM - DA_ROPE), F32)
    zeros = jnp.zeros((lp, DA_DIM - DA_ROPE), F32)
    zh = jnp.zeros((lp, half), F32)
    rc = jnp.concatenate([cos, cos, ones], axis=1)
    rs1 = jnp.concatenate([-sin, zh, zeros], axis=1)
    rs2 = jnp.concatenate([zh, sin, zeros], axis=1)
    return tuple(jnp.concatenate([t, t], axis=1) for t in (rc, rs1, rs2))


def kernel(x, meta_tokens, norm_mix, norm_ffn, norm_final, w_in, pool_w, pool_scale, da_lambda, da_subln,
           rw_mu, rw_w0, rw_w2, rw_a0, rw_a2, rw_g2, rw_k_k, rw_k_a, rw_r_k, rw_lnx_w, rw_lnx_b,
           w_branch, w_out, ffn_up, ffn_conv, ffn_down):
    batch, seq, d = x.shape
    depth = w_in.shape[0]
    assert d == D_MODEL and seq % ATTN_TILE == 0 and meta_tokens.shape == (N_META, D_MODEL)
    lp = seq + ATTN_TILE
    n = batch * lp

    meta = jnp.broadcast_to(meta_tokens[None].astype(x.dtype), (batch, N_META, d))
    xs = jnp.concatenate([jnp.zeros((batch, PAD_FRONT, d), x.dtype), meta, x], axis=1).reshape(n, d)

    rc, rs1, rs2 = _rope_tables(lp)
    hi = lax.broadcasted_iota(jnp.int32, (LANES, LANES), 0) // RW_HEAD
    hj = lax.broadcasted_iota(jnp.int32, (LANES, LANES), 1) // RW_HEAD
    bd = (hi == hj).astype(BF16)
    lora_zeros = jnp.zeros((RW_LORA_SLAB // 2, RW_WIDTH), BF16)
    row = lambda v: v.reshape(1, -1).astype(F32)

    for l in range(depth):
        lambda_init = 0.8 - 0.6 * math.exp(-0.3 * l)
        w_l = w_in[l].astype(BF16)
        g_mix = row(norm_mix[l])
        u_pool, q, k, vt = _proj_attn(xs, g_mix, w_l[:, 0:1536], w_l[:, 1536:2048].T, rc, rs1, rs2, lp)
        w2p = jnp.concatenate([rw_w2[l].astype(BF16), lora_zeros], axis=0)
        a2p = jnp.concatenate([lora_zeros, rw_a2[l].astype(BF16)], axis=0)
        prep = _rw_prep(xs, g_mix, w_l[:, 2048:2048 + RW_SHIFT_WIDTH], row(rw_mu[l]), row(rw_w0[l]), w2p,
                        row(rw_a0[l]), a2p, rw_g2[l].astype(BF16), row(rw_k_k[l]), row(rw_k_a[l]), bd, lp)
        b_rw = _rwkv(*prep, row(rw_lnx_w[l]), row(rw_lnx_b[l]), row(rw_r_k[l]), bd, batch, lp)
        subln_b = jnp.broadcast_to(da_subln[l].astype(F32)[:, None], (DA_VDIM, ATTN_TILE))
        b_da = _attention(q, k, vt, da_lambda[l].astype(F32), subln_b, lambda_init, batch, lp)
        xs = _merge(xs, g_mix, u_pool, b_da, b_rw, w_l[:, 2048 + RW_SHIFT_WIDTH:], pool_w[l].astype(BF16), row(pool_scale[l]),
                    w_branch[l].astype(BF16), w_out[l].astype(BF16), lp)
        xs = _ffn(xs, row(norm_ffn[l]), ffn_up[l].astype(BF16), ffn_conv[l].astype(F32),
                  ffn_down[l].astype(BF16), lp, g_final=row(norm_final) if l == depth - 1 else None)

    return xs.reshape(batch, lp, d)[:, ATTN_TILE:]
```
